```python
import math
import jax
import jax.numpy as jnp
from jax import lax
import numpy as np

D_MODEL = 2048
BATCH = 8
SEQ = 2048
DEPTH = 2
DEC_BATCH = 128
DEC_SEQ = 8
PAST_LEN = 2048
PAGE_SIZE = 128

HEAD_DIM = 128
MOBA_HEADS = 12
SB_HEADS = 4
AB_HEADS = MOBA_HEADS + SB_HEADS
MOBA_BLOCK = 256
MOBA_TOPK = 3
MOBA_QCHUNK = 32
SB_QBLOCK = 128
C_GROUPS = ((128, 1), (512, 4), (2048, 16))
C_GROUP_HEADS = 4
C_HEADS = C_GROUP_HEADS * len(C_GROUPS)
N_BUCKETS = 32
REL_MAX_DIST = 128
N_BIAS_HEADS = 12
MOE_GROUPS = 4
MOE_EXPERTS_PER_GROUP = 8
MOE_EXPERTS = MOE_GROUPS * MOE_EXPERTS_PER_GROUP
MOE_TOPK = 2
MOE_HIDDEN = 512
MOE_BLOCK = 128
N_AB_LAYERS = (DEPTH + 1) // 2
N_C_LAYERS = DEPTH // 2
ALPHA = (2 * DEPTH) ** 0.25
BETA = (8 * DEPTH) ** -0.25
LN_EPS = 1e-5

kernel_name = 'moba_stickbreak_dilated_hmoe_decoder_step'


def layer_norm(x, g, b):
    xf = x.astype(jnp.float32)
    mu = xf.mean(-1, keepdims=True)
    var = jnp.square(xf - mu).mean(-1, keepdims=True)
    return ((xf - mu) * lax.rsqrt(var + LN_EPS) * g + b).astype(x.dtype)


def rel_bucket(dist):
    exact = N_BUCKETS // 2
    far = jnp.maximum(dist, exact).astype(jnp.float32)
    big = exact + (jnp.log(far / exact) / math.log(REL_MAX_DIST / exact) * (N_BUCKETS - exact)).astype(jnp.int32)
    return jnp.where(dist < exact, dist, jnp.minimum(big, N_BUCKETS - 1))


def moba_blocks(k, v):
    L, H, d = k.shape
    nb = -(-L // MOBA_BLOCK)
    pad = ((0, nb * MOBA_BLOCK - L), (0, 0), (0, 0))
    kb = jnp.pad(k, pad).reshape(nb, MOBA_BLOCK, H, d).transpose(2, 0, 1, 3)
    vb = jnp.pad(v, pad).reshape(nb, MOBA_BLOCK, H, d).transpose(2, 0, 1, 3)
    means = kb[:, :L // MOBA_BLOCK].astype(jnp.float32).mean(axis=2)
    return kb, vb, means


def moba_core(q, qpos, kb, vb, means, rel_bias):
    t_len, n_heads, d = q.shape
    n_full = means.shape[1]
    qblk = qpos // MOBA_BLOCK
    own = jnp.broadcast_to(qblk[:, None, None], (t_len, n_heads, 1))
    if n_full > 0:
        gate = jnp.einsum('thd,hnd->thn', q.astype(jnp.float32), means)
        past = jnp.arange(n_full)[None, None, :] < qblk[:, None, None]
        gate = jnp.where(past, gate, -jnp.inf)
        _, sel = lax.top_k(gate, min(MOBA_TOPK, n_full))
        blk_idx = jnp.concatenate([sel.astype(own.dtype), own], axis=-1)
        blk_ok = jnp.concatenate([sel < qblk[:, None, None], jnp.ones(own.shape, bool)], axis=-1)
    else:
        blk_idx, blk_ok = own, jnp.ones(own.shape, bool)
    head = jnp.arange(n_heads)[None, :, None]
    kg = kb[head, blk_idx]
    vg = vb[head, blk_idx]
    kpos = blk_idx[..., None] * MOBA_BLOCK + jnp.arange(MOBA_BLOCK)
    dist = qpos[:, None, None, None] - kpos
    bias = rel_bias[rel_bucket(jnp.maximum(dist, 0)), head[..., None]]
    logits = jnp.einsum('thd,thjkd->thjk', q, kg).astype(jnp.float32) * (d ** -0.5) + bias
    logits = jnp.where(blk_ok[..., None] & (dist >= 0), logits, -jnp.inf)
    p = jax.nn.softmax(logits.reshape(t_len, n_heads, -1), axis=-1).reshape(logits.shape)
    return jnp.einsum('thjk,thjkd->thd', p.astype(vg.dtype), vg)


def moba_prompt(q, k, v, rel_bias):
    bsz, seq, n_heads, d = q.shape
    kb, vb, means = jax.vmap(moba_blocks)(k, v)
    n_chunk = seq // MOBA_QCHUNK

    def step(i):
        b = i // n_chunk
        c = i % n_chunk
        qc = lax.dynamic_slice_in_dim(q[b], c * MOBA_QCHUNK, MOBA_QCHUNK, axis=0)
        qpos = c * MOBA_QCHUNK + jnp.arange(MOBA_QCHUNK)
        return moba_core(qc, qpos, kb[b], vb[b], means[b], rel_bias)

    out = lax.map(step, jnp.arange(bsz * n_chunk))
    return out.reshape(bsz, seq, n_heads, d)


def sb_core(q, qpos, k, v):
    L = k.shape[0]
    t_len, n_heads, d = q.shape
    z = jnp.einsum('thd,lhd->htl', q, k).astype(jnp.float32) * (d ** -0.5)
    mask = jnp.arange(L)[None, None, :] < qpos[None, :, None]
    log_keep = jnp.where(mask, jax.nn.log_sigmoid(-z), 0.0)
    rev = lax.cumsum(log_keep, axis=2, reverse=True)
    after = jnp.concatenate([rev[..., 1:], jnp.zeros((n_heads, t_len, 1), jnp.float32)], axis=-1)
    w = jnp.where(mask, jnp.exp(jax.nn.log_sigmoid(z) + after), 0.0)
    return jnp.einsum('htl,lhd->thd', w.astype(v.dtype), v)


def sb_prompt(q, k, v):
    bsz, seq, n_heads, d = q.shape
    n_blk = seq // SB_QBLOCK

    def step(c):
        qc = lax.dynamic_slice_in_dim(q, c * SB_QBLOCK, SB_QBLOCK, axis=1)
        qpos = c * SB_QBLOCK + jnp.arange(SB_QBLOCK)
        return jax.vmap(sb_core, in_axes=(0, None, 0, 0))(qc, qpos, k, v)

    out = lax.map(step, jnp.arange(n_blk))
    return out.transpose(1, 0, 2, 3, 4).reshape(bsz, seq, n_heads, d)


def ab_sample(q, kv_new, cache_kv_ab, li, page_table, rel_bias):
    t_len = q.shape[1]
    qpos = PAST_LEN + jnp.arange(t_len)

    def per_seq(args):
        q_b, kv_b, pages = args
        kv_past = cache_kv_ab[li, pages]
        kv_all = jnp.concatenate([kv_past.reshape((-1,) + kv_past.shape[2:]), kv_b], axis=0)
        k_all, v_all = kv_all[:, 0], kv_all[:, 1]
        kb, vb, means = moba_blocks(k_all[:, :MOBA_HEADS], v_all[:, :MOBA_HEADS])
        o_a = moba_core(q_b[:, :MOBA_HEADS], qpos, kb, vb, means, rel_bias)
        o_b = sb_core(q_b[:, MOBA_HEADS:], qpos, k_all[:, MOBA_HEADS:], v_all[:, MOBA_HEADS:])
        return jnp.concatenate([o_a, o_b], axis=1)

    return lax.map(per_seq, (q, kv_new, page_table))


def dilated_group_prompt(q, k, v, win, dil, bias_g):
    bsz, seq, n_heads, d = q.shape
    m_max = win // dil
    cls_len = -(-seq // dil)
    nb = -(-cls_len // m_max)
    seq_pad = nb * m_max * dil
    pad = ((0, 0), (0, seq_pad - seq), (0, 0), (0, 0))

    def to_cls(x):
        x = jnp.pad(x, pad).reshape(bsz, nb * m_max, dil, n_heads, d).transpose(0, 2, 1, 3, 4)
        return x.reshape(bsz, dil, nb, m_max, n_heads, d)

    def with_prev(x):
        prev = jnp.concatenate([jnp.zeros_like(x[:, :, :1]), x[:, :, :-1]], axis=2)
        return jnp.concatenate([prev, x], axis=3)

    qc = to_cls(q)
    kk = with_prev(to_cls(k))
    vv = with_prev(to_cls(v))
    qi = jnp.arange(m_max)[:, None]
    kj = jnp.arange(2 * m_max)[None, :]
    steps = qi + m_max - kj
    band = (steps >= 0) & (steps <= m_max)
    mask = band[None] & ((jnp.arange(nb)[:, None, None] > 0) | (kj[None] >= m_max))
    bias = bias_g[rel_bucket(jnp.clip(steps, 0, m_max) * dil)].transpose(2, 0, 1)
    logits = jnp.einsum('brnqhd,brnkhd->brnhqk', qc, kk).astype(jnp.float32) * (d ** -0.5) + bias
    logits = jnp.where(mask[:, None], logits, -jnp.inf)
    lse = jax.nn.logsumexp(logits, axis=-1)
    p = jnp.exp(logits - lse[..., None])
    o = jnp.einsum('brnhqk,brnkhd->brnqhd', p.astype(v.dtype), vv)
    o = o.reshape(bsz, dil, nb * m_max, n_heads, d).transpose(0, 2, 1, 3, 4).reshape(bsz, seq_pad, n_heads, d)[:, :seq]
    lse = lse.transpose(0, 1, 2, 4, 3).reshape(bsz, dil, nb * m_max, n_heads)
    lse = lse.transpose(0, 2, 1, 3).reshape(bsz, seq_pad, n_heads)[:, :seq]
    return o, lse


def dilated_group_sample(q, buf, kv_new, win, dil, bias_g):
    n_buf = buf.shape[1]
    t_len, d = q.shape[1], q.shape[-1]
    m_max = win // dil
    kv = jnp.concatenate([buf, kv_new], axis=1)
    steps = jnp.arange(m_max + 1)
    idx = n_buf + jnp.arange(t_len)[:, None] - steps[None, :] * dil
    valid = idx >= 0
    g = kv[:, jnp.maximum(idx, 0)]
    bias = bias_g[rel_bucket(steps * dil)].T
    logits = jnp.einsum('bthd,btmhd->bthm', q, g[:, :, :, 0]).astype(jnp.float32) * (d ** -0.5) + bias
    logits = jnp.where(valid[:, None, :], logits, -jnp.inf)
    lse = jax.nn.logsumexp(logits, axis=-1)
    p = jnp.exp(logits - lse[..., None])
    o = jnp.einsum('bthm,btmhd->bthd', p.astype(q.dtype), g[:, :, :, 1])
    keep = min(win, n_buf + t_len)
    return o, lse, kv[:, n_buf + t_len - keep:]


def merge_by_denominator(outs, lses):
    wts = jax.nn.softmax(jnp.stack(lses, 0), axis=0)
    return jnp.sum(wts[..., None].astype(outs[0].dtype) * jnp.stack(outs, 0), axis=0)


def hier_moe(x2d, w_rg, b_rg, w_re, b_re, wg, wu, wd):
    n_tok, dm = x2d.shape
    n_grp, epg = w_re.shape[0], w_re.shape[2]
    n_exp = n_grp * epg
    g_logits = jnp.dot(x2d, w_rg).astype(jnp.float32) + b_rg
    grp = jnp.argmax(g_logits, axis=-1)
    p_grp = jnp.take_along_axis(jax.nn.softmax(g_logits, axis=-1), grp[:, None], axis=-1)[:, 0]
    e_logits = jnp.einsum('nd,gde->nge', x2d, w_re).astype(jnp.float32) + b_re[None]
    e_logits = jnp.take_along_axis(e_logits, grp[:, None, None], axis=1)[:, 0]
    top_v, top_i = lax.top_k(e_logits, MOE_TOPK)
    gates = p_grp[:, None] * jax.nn.softmax(top_v, axis=-1)
    expert = (grp[:, None] * epg + top_i).reshape(-1).astype(jnp.int32)
    tok = jnp.repeat(jnp.arange(n_tok, dtype=jnp.int32), MOE_TOPK)
    gate = gates.reshape(-1)
    n_slot = n_tok * MOE_TOPK
    order = jnp.argsort(expert)
    e_s, t_s, g_s = expert[order], tok[order], gate[order]
    counts = jnp.zeros((n_exp,), jnp.int32).at[expert].add(1)
    starts = jnp.cumsum(counts) - counts
    pcounts = (counts + MOE_BLOCK - 1) // MOE_BLOCK * MOE_BLOCK
    pends = jnp.cumsum(pcounts)
    pstarts = pends - pcounts
    dest = pstarts[e_s] + jnp.arange(n_slot, dtype=jnp.int32) - starts[e_s]
    n_blk = -(-(n_slot + n_exp * (MOE_BLOCK - 1)) // MOE_BLOCK)
    slot_tok = jnp.zeros((n_blk * MOE_BLOCK,), jnp.int32).at[dest].set(t_s)
    slot_gate = jnp.zeros((n_blk * MOE_BLOCK,), jnp.float32).at[dest].set(g_s)
    blk_exp = jnp.minimum(jnp.searchsorted(pends, jnp.arange(n_blk, dtype=jnp.int32) * MOE_BLOCK, side='right'), n_exp - 1)

    def expert_block(args):
        b, e = args
        xb = x2d[lax.dynamic_slice_in_dim(slot_tok, b * MOE_BLOCK, MOE_BLOCK)]
        h = jax.nn.silu(xb @ wg[e]) * (xb @ wu[e])
        return h @ wd[e]

    out = lax.map(expert_block, (jnp.arange(n_blk), blk_exp)).reshape(-1, dm)
    y = jnp.zeros((n_tok, dm), jnp.float32).at[slot_tok].add(out.astype(jnp.float32) * slot_gate[:, None])
    return y.astype(x2d.dtype)


def setup_inputs(seed: int = 0) -> dict:
    key = jax.random.key(seed)
    ks = jax.random.split(key, 24)
    n_pages = PAST_LEN // PAGE_SIZE
    n_used = DEC_BATCH * n_pages
    n_pool = (5 * n_used + 3) // 4

    def nrm(k, shape, scale=1.0):
        return scale * jax.random.normal(k, shape, jnp.float32)

    page_table = jax.random.permutation(ks[3], n_pool)[:n_used].reshape(DEC_BATCH, n_pages).astype(jnp.int32)
    c_shapes = [(N_C_LAYERS, DEC_BATCH, min(w, PAST_LEN), 2, C_GROUP_HEADS, HEAD_DIM) for w, _ in C_GROUPS]
    return {
        'x_prompt': nrm(ks[0], (BATCH, SEQ, D_MODEL)),
        'x_sample': nrm(ks[1], (DEC_BATCH, DEC_SEQ, D_MODEL)),
        'cache_kv_ab': nrm(ks[2], (N_AB_LAYERS, n_pool, PAGE_SIZE, 2, AB_HEADS, HEAD_DIM)),
        'state_kv_c_w128': nrm(ks[4], c_shapes[0]),
        'state_kv_c_w512': nrm(ks[5], c_shapes[1]),
        'state_kv_c_w2048': nrm(ks[6], c_shapes[2]),
        'page_table': page_table,
        'rel_bias': nrm(ks[7], (N_BUCKETS, N_BIAS_HEADS), 0.5),
        'w_in_ab': nrm(ks[8], (N_AB_LAYERS, D_MODEL, 3 * AB_HEADS * HEAD_DIM), D_MODEL ** -0.5),
        'w_out_ab': nrm(ks[9], (N_AB_LAYERS, AB_HEADS * HEAD_DIM, D_MODEL), BETA * (AB_HEADS * HEAD_DIM) ** -0.5),
        'w_in_c': nrm(ks[10], (N_C_LAYERS, D_MODEL, 3 * C_HEADS * HEAD_DIM), D_MODEL ** -0.5),
        'w_out_c': nrm(ks[11], (N_C_LAYERS, C_GROUP_HEADS * HEAD_DIM, D_MODEL), BETA * (C_GROUP_HEADS * HEAD_DIM) ** -0.5),
        'ln_mix_g': 1.0 + nrm(ks[12], (DEPTH, D_MODEL), 0.02),
        'ln_mix_b': nrm(ks[13], (DEPTH, D_MODEL), 0.02),
        'w_route_grp': nrm(ks[14], (DEPTH, D_MODEL, MOE_GROUPS), D_MODEL ** -0.5),
        'b_route_grp': nrm(ks[15], (DEPTH, MOE_GROUPS), 0.01),
        'w_route_exp': nrm(ks[16], (DEPTH, MOE_GROUPS, D_MODEL, MOE_EXPERTS_PER_GROUP), D_MODEL ** -0.5),
        'b_route_exp': nrm(ks[17], (DEPTH, MOE_GROUPS, MOE_EXPERTS_PER_GROUP), 0.01),
        'w_gate': nrm(ks[18], (DEPTH, MOE_EXPERTS, D_MODEL, MOE_HIDDEN), D_MODEL ** -0.5),
        'w_up': nrm(ks[19], (DEPTH, MOE_EXPERTS, D_MODEL, MOE_HIDDEN), D_MODEL ** -0.5),
        'w_down': nrm(ks[20], (DEPTH, MOE_EXPERTS, MOE_HIDDEN, D_MODEL), BETA * MOE_HIDDEN ** -0.5),
        'ln_ffn_g': 1.0 + nrm(ks[21], (DEPTH, D_MODEL), 0.02),
        'ln_ffn_b': nrm(ks[22], (DEPTH, D_MODEL), 0.02),
    }


def reference(x_prompt, x_sample, cache_kv_ab, state_kv_c_w128, state_kv_c_w512, state_kv_c_w2048,
              page_table, rel_bias, w_in_ab, w_out_ab, w_in_c, w_out_c, ln_mix_g, ln_mix_b,
              w_route_grp, b_route_grp, w_route_exp, b_route_exp, w_gate, w_up, w_down,
              ln_ffn_g, ln_ffn_b):
    c_states = (state_kv_c_w128, state_kv_c_w512, state_kv_c_w2048)
    xp, xs = x_prompt, x_sample
    bsz, seq, dm = xp.shape
    dbsz, dseq, _ = xs.shape
    kv_ab_p, kv_ab_s = [], []
    c_p = [[] for _ in C_GROUPS]
    c_s = [[] for _ in C_GROUPS]
    for layer in range(DEPTH):
        li = layer // 2
        if layer % 2 == 0:
            hp = (xp @ w_in_ab[li]).reshape(bsz, seq, 3, AB_HEADS, HEAD_DIM)
            qp, kp, vp = hp[:, :, 0], hp[:, :, 1], hp[:, :, 2]
            o_a = moba_prompt(qp[:, :, :MOBA_HEADS], kp[:, :, :MOBA_HEADS], vp[:, :, :MOBA_HEADS], rel_bias)
            o_b = sb_prompt(qp[:, :, MOBA_HEADS:], kp[:, :, MOBA_HEADS:], vp[:, :, MOBA_HEADS:])
            mix_p = jnp.concatenate([o_a, o_b], axis=2).reshape(bsz, seq, -1) @ w_out_ab[li]
            kv_ab_p.append(hp[:, :, 1:])
            hs = (xs @ w_in_ab[li]).reshape(dbsz, dseq, 3, AB_HEADS, HEAD_DIM)
            o_s = ab_sample(hs[:, :, 0], hs[:, :, 1:], cache_kv_ab, li, page_table, rel_bias)
            mix_s = o_s.reshape(dbsz, dseq, -1) @ w_out_ab[li]
            kv_ab_s.append(hs[:, :, 1:])
        else:
            hp = (xp @ w_in_c[li]).reshape(bsz, seq, 3, C_HEADS, HEAD_DIM)
            hs = (xs @ w_in_c[li]).reshape(dbsz, dseq, 3, C_HEADS, HEAD_DIM)
            outs_p, lses_p, outs_s, lses_s = [], [], [], []
            for gi, (win, dil) in enumerate(C_GROUPS):
                sl = slice(gi * C_GROUP_HEADS, (gi + 1) * C_GROUP_HEADS)
                bias_g = rel_bias[:, sl]
                o, lse = dilated_group_prompt(hp[:, :, 0, sl], hp[:, :, 1, sl], hp[:, :, 2, sl], win, dil, bias_g)
                outs_p.append(o)
                lses_p.append(lse)
                c_p[gi].append(hp[:, seq - min(win, seq):, 1:, sl])
                o, lse, buf = dilated_group_sample(hs[:, :, 0, sl], c_states[gi][li], hs[:, :, 1:, sl], win, dil, bias_g)
                outs_s.append(o)
                lses_s.append(lse)
                c_s[gi].append(buf)
            mix_p = merge_by_denominator(outs_p, lses_p).reshape(bsz, seq, -1) @ w_out_c[li]
            mix_s = merge_by_denominator(outs_s, lses_s).reshape(dbsz, dseq, -1) @ w_out_c[li]
        xp = layer_norm(ALPHA * xp + mix_p, ln_mix_g[layer], ln_mix_b[layer])
        xs = layer_norm(ALPHA * xs + mix_s, ln_mix_g[layer], ln_mix_b[layer])
        moe_w = (w_route_grp[layer], b_route_grp[layer], w_route_exp[layer], b_route_exp[layer],
                 w_gate[layer], w_up[layer], w_down[layer])
        f_p = hier_moe(xp.reshape(-1, dm), *moe_w).reshape(xp.shape)
        f_s = hier_moe(xs.reshape(-1, dm), *moe_w).reshape(xs.shape)
        xp = layer_norm(ALPHA * xp + f_p, ln_ffn_g[layer], ln_ffn_b[layer])
        xs = layer_norm(ALPHA * xs + f_s, ln_ffn_g[layer], ln_ffn_b[layer])
    return (xp, xs, jnp.stack(kv_ab_p), jnp.stack(kv_ab_s), jnp.stack(c_p[0]), jnp.stack(c_s[0]),
            jnp.stack(c_p[1]), jnp.stack(c_s[1]), jnp.stack(c_p[2]), jnp.stack(c_s[2]))
```

```python
import functools
import math

import jax
import jax.numpy as jnp
from jax import lax
from jax.experimental import pallas as pl
from jax.experimental.pallas import tpu as pltpu

F32 = jnp.float32
BF16 = jnp.bfloat16
NEG_INF = float("-inf")
HIGHEST = lax.Precision.HIGHEST

HEAD_DIM = 128
MOBA_HEADS = 12
SB_HEADS = 4
AB_HEADS = MOBA_HEADS + SB_HEADS
MOBA_BLOCK = 256
MOBA_TOPK = 3
PAGE_SIZE = 128
C_GROUPS = ((128, 1), (512, 4), (2048, 16))
C_GROUP_HEADS = 4
C_HEADS = C_GROUP_HEADS * len(C_GROUPS)
N_BUCKETS = 32
REL_MAX_DIST = 128
MOE_TOPK = 2
DEPTH = 2
ALPHA = (2 * DEPTH) ** 0.25
LN_EPS = 1e-5
SCALE = HEAD_DIM ** -0.5

QBLK = 256
MOE_ROWS = 256
C_SLABS = 144
VMEM_LIMIT = 56 * 1024 * 1024


def _dot(a, b, **kw):
    return jnp.dot(a, b, preferred_element_type=F32, **kw)


def _dot_nt(a, b, **kw):
    return lax.dot_general(a, b, (((1,), (1,)), ((), ())), preferred_element_type=F32, **kw)


def _split_dot(x, u):
    hi = x.astype(BF16)
    lo = (x - hi.astype(F32)).astype(BF16)
    return _dot(hi, u) + _dot(lo, u)


def _log_sigmoid(z):
    return jnp.minimum(z, 0.0) - jnp.log(1.0 + jnp.exp(-jnp.abs(z)))


def _params(*sem):
    return pltpu.CompilerParams(dimension_semantics=sem, vmem_limit_bytes=VMEM_LIMIT)


def _rel_bucket(dist):
    exact = N_BUCKETS // 2
    far = jnp.maximum(dist, exact).astype(F32)
    big = exact + (jnp.log(far / exact) / math.log(REL_MAX_DIST / exact) * (N_BUCKETS - exact)).astype(jnp.int32)
    return jnp.where(dist < exact, dist, jnp.minimum(big, N_BUCKETS - 1))


def _mm_kernel(x_ref, w_ref, o_ref):
    o_ref[...] = _dot(x_ref[...].astype(BF16), w_ref[...].astype(BF16))


def _matmul(x, w, col0, cstep, ncols, tn=512):
    m, k = x.shape
    tm = min(m, 1024)
    return pl.pallas_call(
        _mm_kernel,
        grid=(m // tm, ncols // tn),
        in_specs=[pl.BlockSpec((tm, k), lambda i, j: (i, 0)),
                  pl.BlockSpec((k, tn), lambda i, j: (0, col0 + j * cstep))],
        out_specs=pl.BlockSpec((tm, tn), lambda i, j: (i, j)),
        out_shape=jax.ShapeDtypeStruct((m, ncols), F32),
        compiler_params=_params("parallel", "parallel"),
        name="matmul",
    )(x, w)


def _res_ln_kernel(n_add, x_ref, *refs):
    adds, (g_ref, b_ref, o_ref) = refs[:n_add], refs[n_add:]
    y = ALPHA * x_ref[...]
    for a in adds:
        y = y + a[...]
    mu = jnp.mean(y, axis=-1, keepdims=True)
    yc = y - mu
    var = jnp.mean(yc * yc, axis=-1, keepdims=True)
    o_ref[...] = yc * lax.rsqrt(var + LN_EPS) * g_ref[...] + b_ref[...]


def _res_ln(x, adds, g, b):
    m, d = x.shape
    tm = min(m, 512)
    row = pl.BlockSpec((tm, d), lambda i: (i, 0))
    vec = pl.BlockSpec((1, d), lambda i: (0, 0))
    return pl.pallas_call(
        functools.partial(_res_ln_kernel, len(adds)),
        grid=(m // tm,),
        in_specs=[row] * (1 + len(adds)) + [vec, vec],
        out_specs=row,
        out_shape=jax.ShapeDtypeStruct((m, d), F32),
        compiler_params=_params("parallel"),
        name="res_ln",
    )(x, *adds, g.reshape(1, d), b.reshape(1, d))


def _ab_prompt_kernel(q_ref, k_ref, v_ref, bias_ref, o_ref, means_ref):
    h = pl.program_id(1)
    i = pl.program_id(2)
    n_blk = k_ref.shape[0] // QBLK
    row = lax.broadcasted_iota(jnp.int32, (QBLK, QBLK), 0)
    col = lax.broadcasted_iota(jnp.int32, (QBLK, QBLK), 1)

    def kv_tile(n):
        start = pl.multiple_of(n * QBLK, QBLK)
        return k_ref[pl.ds(start, QBLK), :].astype(BF16), v_ref[pl.ds(start, QBLK), :].astype(BF16)

    @pl.when(h < MOBA_HEADS)
    def _moba():
        @pl.when(i == 0)
        def _():
            means_ref[...] = jnp.zeros_like(means_ref)
            ksum = jnp.sum(k_ref[...].reshape(n_blk, QBLK, HEAD_DIM), axis=1)
            means_ref[0:n_blk, :] = ksum * (1.0 / QBLK)

        q = q_ref[...]
        gate = _dot_nt(q, means_ref[...], precision=HIGHEST)
        lane = lax.broadcasted_iota(jnp.int32, gate.shape, 1)
        past = lane < i
        sel = jnp.zeros(gate.shape, F32)
        for n in range(n_blk):
            g_n = gate[:, n:n + 1]
            beats = jnp.where(past, jnp.where((gate > g_n) | ((gate == g_n) & (lane < n)), 1.0, 0.0), 0.0)
            cnt = jnp.sum(beats, axis=1, keepdims=True)
            sel = jnp.where(lane == n, jnp.where(cnt < MOBA_TOPK, 1.0, 0.0), sel)

        qs = (q * SCALE).astype(BF16)
        k_t, v_t = kv_tile(i)
        s = _dot_nt(qs, k_t) + bias_ref[0, 0]
        s = jnp.where(col <= row, s, NEG_INF)
        m = jnp.max(s, axis=1, keepdims=True)
        p = jnp.exp(s - m)
        l = jnp.sum(p, axis=1, keepdims=True)
        acc = _dot(p.astype(BF16), v_t)

        def body(n, carry):
            m, l, acc = carry
            k_t, v_t = kv_tile(n)
            s = _dot_nt(qs, k_t) + bias_ref[0, jnp.minimum(i - n, 2)]
            sel_n = jnp.max(jnp.where(lane == n, sel, 0.0), axis=1, keepdims=True)
            s = jnp.where(sel_n > 0.5, s, NEG_INF)
            m_new = jnp.maximum(m, jnp.max(s, axis=1, keepdims=True))
            a = jnp.exp(m - m_new)
            p = jnp.exp(s - m_new)
            l = a * l + jnp.sum(p, axis=1, keepdims=True)
            acc = a * acc + _dot(p.astype(BF16), v_t)
            return m_new, l, acc

        m, l, acc = lax.fori_loop(0, i, body, (m, l, acc))
        o_ref[...] = acc / l

    @pl.when(h >= MOBA_HEADS)
    def _stick_breaking():
        qs = (q_ref[...] * SCALE).astype(BF16)
        upper = jnp.where(row > col, 1.0, 0.0).astype(BF16)

        def tile(n, valid, carry, acc):
            k_t, v_t = kv_tile(n)
            z = _dot_nt(qs, k_t)
            ls = _log_sigmoid(z)
            lk = ls - z
            if valid is not None:
                lk = jnp.where(valid, lk, 0.0)
            w = jnp.exp(ls + _split_dot(lk, upper) + carry)
            if valid is not None:
                w = jnp.where(valid, w, 0.0)
            acc = acc + _dot(w.astype(BF16), v_t)
            return carry + jnp.sum(lk, axis=1, keepdims=True), acc

        carry, acc = tile(i, col < row, jnp.zeros((QBLK, 1), F32), jnp.zeros((QBLK, HEAD_DIM), F32))
        carry, acc = lax.fori_loop(0, i, lambda j, c: tile(i - 1 - j, None, *c), (carry, acc))
        o_ref[...] = acc


def _ab_prompt(q, kv, bias_tiles, bsz, seq):
    nq = seq // QBLK
    return pl.pallas_call(
        _ab_prompt_kernel,
        grid=(bsz, AB_HEADS, nq),
        in_specs=[pl.BlockSpec((QBLK, HEAD_DIM), lambda b, h, i: (b * nq + i, h)),
                  pl.BlockSpec((seq, HEAD_DIM), lambda b, h, i: (b, h)),
                  pl.BlockSpec((seq, HEAD_DIM), lambda b, h, i: (b, AB_HEADS + h)),
                  pl.BlockSpec((1, 3, QBLK, QBLK), lambda b, h, i: (jnp.minimum(h, MOBA_HEADS - 1), 0, 0, 0))],
        out_specs=pl.BlockSpec((QBLK, HEAD_DIM), lambda b, h, i: (b * nq + i, h)),
        out_shape=jax.ShapeDtypeStruct(q.shape, F32),
        scratch_shapes=[pltpu.VMEM((HEAD_DIM, HEAD_DIM), F32)],
        compiler_params=_params("parallel", "parallel", "arbitrary"),
        name="ab_prompt",
    )(q, kv, kv, bias_tiles)


def _moba_prompt_bias(rel_bias):
    t = jnp.arange(QBLK)
    dist = jnp.arange(3)[:, None, None] * QBLK + t[None, :, None] - t[None, None, :]
    tiles = rel_bias[_rel_bucket(jnp.maximum(dist, 0))]
    return tiles.transpose(3, 0, 1, 2)


def _ab_sample_kernel(pt_ref, q_ref, kvn_ref, page_ref, bias_ref, o_ref,
                      kpad, vpad, m_sc, l_sc, acc_sc, bsum, sb_carry, sb_acc):
    del pt_ref
    s = pl.program_id(1)
    n_steps = bias_ref.shape[1]
    t_len = q_ref.shape[0]
    n_blocks = (n_steps - 1) * PAGE_SIZE // MOBA_BLOCK
    pages_per_block = MOBA_BLOCK // PAGE_SIZE
    rows_per_pos = 2 * AB_HEADS
    row = lax.broadcasted_iota(jnp.int32, (t_len, PAGE_SIZE), 0)
    col = lax.broadcasted_iota(jnp.int32, (t_len, PAGE_SIZE), 1)
    eye_r = lax.broadcasted_iota(jnp.int32, (PAGE_SIZE, PAGE_SIZE), 0)
    eye_c = lax.broadcasted_iota(jnp.int32, (PAGE_SIZE, PAGE_SIZE), 1)
    upper = jnp.where(eye_r > eye_c, 1.0, 0.0).astype(BF16)

    def head_q(h):
        return q_ref[:, h * HEAD_DIM:(h + 1) * HEAD_DIM]

    def moba_part(h, k_h, v_h, valid):
        sc = _dot_nt(head_q(h) * SCALE, k_h) + bias_ref[h, s]
        if valid is not None:
            sc = jnp.where(valid, sc, NEG_INF)
        m = jnp.max(sc, axis=1, keepdims=True)
        e = jnp.exp(sc - m)
        m_sc[h, s] = jnp.broadcast_to(m, (t_len, HEAD_DIM))
        l_sc[h, s] = jnp.broadcast_to(jnp.sum(e, axis=1, keepdims=True), (t_len, HEAD_DIM))
        acc_sc[h, s] = _dot(e, v_h)

    def sb_part(j, k_h, v_h, valid):
        z = _dot_nt(head_q(MOBA_HEADS + j) * SCALE, k_h)
        ls = _log_sigmoid(z)
        lk = ls - z
        if valid is not None:
            lk = jnp.where(valid, lk, 0.0)
        w = jnp.exp(ls + _split_dot(lk, upper) + sb_carry[j])
        if valid is not None:
            w = jnp.where(valid, w, 0.0)
        sb_acc[j] = sb_acc[j] + _dot(w, v_h)
        sb_carry[j] = sb_carry[j] + jnp.sum(lk, axis=1, keepdims=True)

    @pl.when(s == 0)
    def _new_tokens():
        kpad[...] = jnp.zeros_like(kpad)
        vpad[...] = jnp.zeros_like(vpad)
        bsum[...] = jnp.zeros_like(bsum)
        sb_carry[...] = jnp.zeros_like(sb_carry)
        sb_acc[...] = jnp.zeros_like(sb_acc)
        for h in range(AB_HEADS):
            kpad[0:t_len, :] = kvn_ref[:, h * HEAD_DIM:(h + 1) * HEAD_DIM]
            vpad[0:t_len, :] = kvn_ref[:, (AB_HEADS + h) * HEAD_DIM:(AB_HEADS + h + 1) * HEAD_DIM]
            if h < MOBA_HEADS:
                moba_part(h, kpad[...], vpad[...], col <= row)
            else:
                sb_part(h - MOBA_HEADS, kpad[...], vpad[...], col < row)

    @pl.when(s > 0)
    def _page():
        blk = (n_steps - 1 - s) // pages_per_block
        for h in range(AB_HEADS):
            k_h = page_ref[0, pl.ds(h, PAGE_SIZE, stride=rows_per_pos), :]
            v_h = page_ref[0, pl.ds(AB_HEADS + h, PAGE_SIZE, stride=rows_per_pos), :]
            if h < MOBA_HEADS:
                bsum[h, blk] = bsum[h, blk] + jnp.sum(k_h.reshape(PAGE_SIZE // 8, 8, HEAD_DIM), axis=0)
                moba_part(h, k_h, v_h, None)
            else:
                sb_part(h - MOBA_HEADS, k_h, v_h, None)

    @pl.when(s == n_steps - 1)
    def _finish():
        lane = lax.broadcasted_iota(jnp.int32, (t_len, HEAD_DIM), 1)
        fold = jnp.where(eye_r // 8 == eye_c, 1.0, 0.0)
        for h in range(MOBA_HEADS):
            part = _dot_nt(head_q(h), bsum[h].reshape(HEAD_DIM, HEAD_DIM), precision=HIGHEST)
            gate = _dot(part, fold, precision=HIGHEST) * (1.0 / MOBA_BLOCK)
            valid = lane < n_blocks
            m_all = m_sc[h, 0]
            sels = []
            for n in range(n_blocks):
                g_n = gate[:, n:n + 1]
                beats = jnp.where(valid, jnp.where((gate > g_n) | ((gate == g_n) & (lane < n)), 1.0, 0.0), 0.0)
                sel_n = jnp.sum(beats, axis=1, keepdims=True) < MOBA_TOPK
                sels.append(sel_n)
                for pg in range(pages_per_block):
                    st = n_steps - 1 - (n * pages_per_block + pg)
                    m_all = jnp.maximum(m_all, jnp.where(sel_n, m_sc[h, st], NEG_INF))
            wgt = jnp.exp(m_sc[h, 0] - m_all)
            num = wgt * acc_sc[h, 0]
            den = wgt * l_sc[h, 0]
            for n in range(n_blocks):
                for pg in range(pages_per_block):
                    st = n_steps - 1 - (n * pages_per_block + pg)
                    wgt = jnp.where(sels[n], jnp.exp(m_sc[h, st] - m_all), 0.0)
                    num = num + wgt * acc_sc[h, st]
                    den = den + wgt * l_sc[h, st]
            o_ref[:, h * HEAD_DIM:(h + 1) * HEAD_DIM] = num / den
        for j in range(SB_HEADS):
            o_ref[:, (MOBA_HEADS + j) * HEAD_DIM:(MOBA_HEADS + j + 1) * HEAD_DIM] = sb_acc[j]


def _ab_sample(q, kvn, cache, page_table, bias, t_len):
    nseq, n_pages = page_table.shape
    n_steps = n_pages + 1
    d_all = AB_HEADS * HEAD_DIM

    def page_map(b, s, pt):
        return (pt[b * n_pages + n_pages - jnp.maximum(s, 1)], 0, 0)

    grid_spec = pltpu.PrefetchScalarGridSpec(
        num_scalar_prefetch=1,
        grid=(nseq, n_steps),
        in_specs=[pl.BlockSpec((t_len, d_all), lambda b, s, pt: (b, 0)),
                  pl.BlockSpec((t_len, 2 * d_all), lambda b, s, pt: (b, 0)),
                  pl.BlockSpec((1, PAGE_SIZE * 2 * AB_HEADS, HEAD_DIM), page_map),
                  pl.BlockSpec((MOBA_HEADS, n_steps, t_len, HEAD_DIM), lambda b, s, pt: (0, 0, 0, 0))],
        out_specs=pl.BlockSpec((t_len, d_all), lambda b, s, pt: (b, 0)),
        scratch_shapes=[pltpu.VMEM((PAGE_SIZE, HEAD_DIM), F32),
                        pltpu.VMEM((PAGE_SIZE, HEAD_DIM), F32),
                        pltpu.VMEM((MOBA_HEADS, n_steps, t_len, HEAD_DIM), F32),
                        pltpu.VMEM((MOBA_HEADS, n_steps, t_len, HEAD_DIM), F32),
                        pltpu.VMEM((MOBA_HEADS, n_steps, t_len, HEAD_DIM), F32),
                        pltpu.VMEM((MOBA_HEADS, HEAD_DIM // 8, 8, HEAD_DIM), F32),
                        pltpu.VMEM((SB_HEADS, t_len, HEAD_DIM), F32),
                        pltpu.VMEM((SB_HEADS, t_len, HEAD_DIM), F32)],
    )
    return pl.pallas_call(
        _ab_sample_kernel,
        grid_spec=grid_spec,
        out_shape=jax.ShapeDtypeStruct((nseq * t_len, d_all), F32),
        compiler_params=_params("parallel", "arbitrary"),
        name="ab_sample",
    )(page_table.reshape(-1), q, kvn, cache, bias)


def _moba_sample_bias(rel_bias, n_pages, t_len, past_len):
    c = jnp.arange(PAGE_SIZE)
    page = n_pages - jnp.arange(1, n_pages + 1)
    kpos = jnp.concatenate([(past_len + c)[None], page[:, None] * PAGE_SIZE + c[None]], axis=0)
    dist = (past_len + jnp.arange(t_len))[None, :, None] - kpos[:, None, :]
    return rel_bias[_rel_bucket(jnp.maximum(dist, 0))].transpose(3, 0, 1, 2)


def _c_tile_counts(seq):
    return tuple(min(win // QBLK + 1, seq // QBLK) if win >= QBLK else 2 for win, _ in C_GROUPS)


def _c_prompt_kernel(counts, *refs):
    q_refs, kv_refs, (mask_ref, o_ref) = refs[0:3], refs[3:9], refs[9:]
    i = pl.program_id(2)
    state = None
    tile0 = 0
    for g, n_tiles in enumerate(counts):
        qs = (q_refs[g][...] * SCALE).astype(BF16)
        k_ref, v_ref = kv_refs[2 * g], kv_refs[2 * g + 1]

        def step(off, carry, qs=qs, k_ref=k_ref, v_ref=v_ref, tile0=tile0):
            start = pl.multiple_of((i - off) * QBLK, QBLK)
            k_t = k_ref[pl.ds(start, QBLK), :].astype(BF16)
            v_t = v_ref[pl.ds(start, QBLK), :].astype(BF16)
            sc = _dot_nt(qs, k_t) + mask_ref[0, tile0 + off]
            if carry is None:
                m_new = jnp.max(sc, axis=1, keepdims=True)
                p = jnp.exp(sc - m_new)
                return m_new, jnp.sum(p, axis=1, keepdims=True), _dot(p.astype(BF16), v_t)
            m, l, acc = carry
            m_new = jnp.maximum(m, jnp.max(sc, axis=1, keepdims=True))
            a = jnp.exp(m - m_new)
            p = jnp.exp(sc - m_new)
            return m_new, a * l + jnp.sum(p, axis=1, keepdims=True), a * acc + _dot(p.astype(BF16), v_t)

        first = 0
        if state is None:
            state = step(0, None)
            first = 1
        state = lax.fori_loop(first, jnp.minimum(i, n_tiles - 1) + 1, step, state)
        tile0 += n_tiles
    m, l, acc = state
    o_ref[...] = acc / l


def _c_prompt(q, kvs, mask_tiles, bsz, seq):
    nq = seq // QBLK
    counts = _c_tile_counts(seq)
    q_specs = [pl.BlockSpec((QBLK, HEAD_DIM), lambda b, j, i, g=g: (b * nq + i, C_GROUP_HEADS * g + j))
               for g in range(len(C_GROUPS))]
    kv_specs = []
    for g in range(len(C_GROUPS)):
        kv_specs.append(pl.BlockSpec((seq, HEAD_DIM), lambda b, j, i: (b, j)))
        kv_specs.append(pl.BlockSpec((seq, HEAD_DIM), lambda b, j, i: (b, C_GROUP_HEADS + j)))
    kv_args = [kvs[g] for g in range(len(C_GROUPS)) for _ in range(2)]
    return pl.pallas_call(
        functools.partial(_c_prompt_kernel, counts),
        grid=(bsz, C_GROUP_HEADS, nq),
        in_specs=q_specs + kv_specs + [pl.BlockSpec((1, sum(counts), QBLK, QBLK), lambda b, j, i: (j, 0, 0, 0))],
        out_specs=pl.BlockSpec((QBLK, HEAD_DIM), lambda b, j, i: (b * nq + i, j)),
        out_shape=jax.ShapeDtypeStruct((bsz * seq, C_GROUP_HEADS * HEAD_DIM), F32),
        compiler_params=_params("parallel", "parallel", "arbitrary"),
        name="c_prompt",
    )(q, q, q, *kv_args, mask_tiles)


def _c_prompt_masks(rel_bias, seq):
    t = jnp.arange(QBLK)
    tiles = []
    for g, ((win, dil), n_tiles) in enumerate(zip(C_GROUPS, _c_tile_counts(seq))):
        d = jnp.arange(n_tiles)[:, None, None] * QBLK + t[None, :, None] - t[None, None, :]
        ok = (d >= 0) & (d <= win) & (d % dil == 0)
        bias = rel_bias[:, C_GROUP_HEADS * g:C_GROUP_HEADS * (g + 1)][_rel_bucket(jnp.maximum(d, 0))]
        tiles.append(jnp.where(ok[..., None], bias, NEG_INF))
    return jnp.concatenate(tiles, axis=0).transpose(3, 0, 1, 2)


def _c_sample_kernel(q_ref, *refs):
    n_g = len(C_GROUPS)
    kvn_refs, st_refs = refs[0:n_g], refs[n_g:2 * n_g]
    mask_ref, o_ref = refs[2 * n_g], refs[2 * n_g + 1]
    cats = refs[2 * n_g + 2:]
    t_len = q_ref.shape[0]
    for g, (win, dil) in enumerate(C_GROUPS):
        n_buf = st_refs[g].shape[2]
        cat = cats[g]
        cat[0:n_buf] = st_refs[g][0, 0]
        cat[n_buf:n_buf + t_len] = kvn_refs[g][...]
        cat[n_buf + t_len:] = jnp.zeros((cat.shape[0] - n_buf - t_len,) + cat.shape[1:], F32)
    for t in range(t_len):
        state = None
        for g, (win, dil) in enumerate(C_GROUPS):
            n_buf = st_refs[g].shape[2]
            m_max = win // dil
            x = cats[g][pl.ds(n_buf + t - m_max * dil, C_SLABS, stride=dil)].reshape(C_SLABS * 8, HEAD_DIM)
            sc = _dot_nt(q_ref[t, g] * SCALE, x) + mask_ref[g]
            if state is None:
                m_new = jnp.max(sc, axis=1, keepdims=True)
                p = jnp.exp(sc - m_new)
                l = jnp.sum(p, axis=1, keepdims=True)
                acc = _dot(pltpu.roll(p, C_GROUP_HEADS, axis=1), x)
            else:
                m, l, acc = state
                m_new = jnp.maximum(m, jnp.max(sc, axis=1, keepdims=True))
                a = jnp.exp(m - m_new)
                p = jnp.exp(sc - m_new)
                l = a * l + jnp.sum(p, axis=1, keepdims=True)
                acc = a * acc + _dot(pltpu.roll(p, C_GROUP_HEADS, axis=1), x)
            state = (m_new, l, acc)
        m, l, acc = state
        o_ref[0, t] = acc / l


def _c_sample(q, kvns, states, masks, t_len):
    nseq = states[0].shape[1]
    n_g = len(C_GROUPS)
    in_specs = [pl.BlockSpec((t_len, n_g, 8, HEAD_DIM), lambda b: (b, 0, 0, 0))]
    in_specs += [pl.BlockSpec((t_len, 8, HEAD_DIM), lambda b: (b, 0, 0)) for _ in range(n_g)]
    in_specs += [pl.BlockSpec((1, 1) + st.shape[2:], lambda b: (0, b, 0, 0, 0)) for st in states]
    in_specs += [pl.BlockSpec(masks.shape, lambda b: (0, 0, 0))]
    scratch = []
    for st, (win, dil) in zip(states, C_GROUPS):
        n_buf = st.shape[2]
        rows = n_buf - win + (C_SLABS - 1) * dil + t_len
        scratch.append(pltpu.VMEM((max(rows, n_buf + t_len), 8, HEAD_DIM), F32))
    return pl.pallas_call(
        _c_sample_kernel,
        grid=(nseq,),
        in_specs=in_specs,
        out_specs=pl.BlockSpec((1, t_len, 8, HEAD_DIM), lambda b: (b, 0, 0, 0)),
        out_shape=jax.ShapeDtypeStruct((nseq, t_len, 8, HEAD_DIM), F32),
        scratch_shapes=scratch,
        compiler_params=_params("parallel"),
        name="c_sample",
    )(q, *kvns, *states, masks)


def _c_sample_masks(rel_bias):
    c = jnp.arange(C_SLABS * 8)
    slab, is_v, head = c // 8, (c % 8) // C_GROUP_HEADS, c % C_GROUP_HEADS
    slot = jnp.arange(8)
    out = []
    for g, (win, dil) in enumerate(C_GROUPS):
        m_max = win // dil
        steps = m_max - slab
        ok = (steps >= 0)[None, :] & (is_v == 0)[None, :] & (head[None, :] == slot[:, None])
        bias = rel_bias[:, C_GROUP_HEADS * g:C_GROUP_HEADS * (g + 1)][_rel_bucket(jnp.maximum(steps, 0) * dil)]
        bias = jnp.concatenate([bias.T, jnp.zeros_like(bias.T)], axis=0)
        out.append(jnp.where(slot[:, None] < C_GROUP_HEADS, jnp.where(ok, bias, NEG_INF), 0.0))
    return jnp.stack(out)


def _router_kernel(n_grp, epg, x_ref, w_ref, b_ref, e_ref, g_ref):
    logits = _dot(x_ref[...], w_ref[...], precision=HIGHEST) + b_ref[...]
    lane = lax.broadcasted_iota(jnp.int32, logits.shape, 1)
    big = jnp.int32(2 ** 30)

    def first_max(valid):
        v = jnp.max(jnp.where(valid, logits, NEG_INF), axis=1, keepdims=True)
        idx = jnp.min(jnp.where(valid & (logits == v), lane, big), axis=1, keepdims=True)
        return v, idx

    in_grp = lane < n_grp
    g_max, grp = first_max(in_grp)
    p_grp = 1.0 / jnp.sum(jnp.where(in_grp, jnp.exp(logits - g_max), 0.0), axis=1, keepdims=True)
    lo = n_grp + grp * epg
    in_exp = (lane >= lo) & (lane < lo + epg)
    v1, i1 = first_max(in_exp)
    v2, i2 = first_max(in_exp & (lane != i1))
    e2 = jnp.exp(v2 - v1)
    g1 = p_grp / (1.0 + e2)
    g2 = p_grp * e2 / (1.0 + e2)
    e_ref[...] = jnp.where(lane == 0, i1 - n_grp, jnp.where(lane == 1, i2 - n_grp, 0))
    g_ref[...] = jnp.where(lane == 0, g1, jnp.where(lane == 1, g2, 0.0))


def _router(x, w_rg, b_rg, w_re, b_re):
    n_tok, dm = x.shape
    n_grp, _, epg = w_re.shape
    n_col = n_grp + n_grp * epg
    w = jnp.concatenate([w_rg, w_re.transpose(1, 0, 2).reshape(dm, n_grp * epg)], axis=1)
    w = jnp.pad(w, ((0, 0), (0, HEAD_DIM - n_col)))
    b = jnp.pad(jnp.concatenate([b_rg, b_re.reshape(-1)]), (0, HEAD_DIM - n_col)).reshape(1, HEAD_DIM)
    tm = min(n_tok, 512)
    row = pl.BlockSpec((tm, HEAD_DIM), lambda i: (i, 0))
    return pl.pallas_call(
        functools.partial(_router_kernel, n_grp, epg),
        grid=(n_tok // tm,),
        in_specs=[pl.BlockSpec((tm, dm), lambda i: (i, 0)),
                  pl.BlockSpec((dm, HEAD_DIM), lambda i: (0, 0)),
                  pl.BlockSpec((1, HEAD_DIM), lambda i: (0, 0))],
        out_specs=[row, row],
        out_shape=[jax.ShapeDtypeStruct((n_tok, HEAD_DIM), jnp.int32), jax.ShapeDtypeStruct((n_tok, HEAD_DIM), F32)],
        compiler_params=_params("parallel"),
        name="router",
    )(x, w, b)


def _ffn_kernel(be_ref, nv_ref, x_ref, wg_ref, wu_ref, wd_ref, gate_ref, o_ref):
    del be_ref

    @pl.when(pl.program_id(0) < nv_ref[0])
    def _():
        xb = x_ref[...].astype(BF16)
        g = _dot(xb, wg_ref[0].astype(BF16))
        u = _dot(xb, wu_ref[0].astype(BF16))
        hid = g * (1.0 / (1.0 + jnp.exp(-g))) * u
        o_ref[...] = _dot(hid.astype(BF16), wd_ref[0].astype(BF16)) * gate_ref[...]

    @pl.when(pl.program_id(0) >= nv_ref[0])
    def _():
        o_ref[...] = jnp.zeros_like(o_ref)


def _expert_ffn(xs, blk_exp, n_valid, row_gate, wg, wu, wd):
    n_rows, dm = xs.shape
    hid = wg.shape[2]
    n_blk = n_rows // MOE_ROWS
    grid_spec = pltpu.PrefetchScalarGridSpec(
        num_scalar_prefetch=2,
        grid=(n_blk,),
        in_specs=[pl.BlockSpec((MOE_ROWS, dm), lambda i, be, nv: (i, 0)),
                  pl.BlockSpec((1, dm, hid), lambda i, be, nv: (be[i], 0, 0)),
                  pl.BlockSpec((1, dm, hid), lambda i, be, nv: (be[i], 0, 0)),
                  pl.BlockSpec((1, hid, dm), lambda i, be, nv: (be[i], 0, 0)),
                  pl.BlockSpec((MOE_ROWS, 1), lambda i, be, nv: (i, 0))],
        out_specs=pl.BlockSpec((MOE_ROWS, dm), lambda i, be, nv: (i, 0)),
    )
    return pl.pallas_call(
        _ffn_kernel,
        grid_spec=grid_spec,
        out_shape=jax.ShapeDtypeStruct((n_rows, dm), F32),
        compiler_params=_params("arbitrary"),
        name="expert_ffn",
    )(blk_exp, n_valid, xs, wg, wu, wd, row_gate)


def _hier_moe(x, w_rg, b_rg, w_re, b_re, wg, wu, wd):
    n_tok, dm = x.shape
    n_exp = wg.shape[0]
    eid, gts = _router(x, w_rg, b_rg, w_re, b_re)
    expert = eid[:, :MOE_TOPK].reshape(-1)
    gate = gts[:, :MOE_TOPK].reshape(-1)
    n_slot = n_tok * MOE_TOPK
    order = jnp.argsort(expert, stable=True).astype(jnp.int32)
    counts = jnp.sum(expert[:, None] == jnp.arange(n_exp, dtype=jnp.int32)[None, :], axis=0, dtype=jnp.int32)
    starts = jnp.cumsum(counts) - counts
    pcounts = (counts + MOE_ROWS - 1) // MOE_ROWS * MOE_ROWS
    pends = jnp.cumsum(pcounts)
    pstarts = pends - pcounts
    n_blk = -(-(n_slot + n_exp * (MOE_ROWS - 1)) // MOE_ROWS)
    blk_start = jnp.arange(n_blk, dtype=jnp.int32) * MOE_ROWS
    blk_exp = jnp.minimum(jnp.searchsorted(pends, blk_start, side="right"), n_exp - 1).astype(jnp.int32)
    n_valid = (pends[-1] // MOE_ROWS).astype(jnp.int32).reshape(1)
    r = jnp.arange(n_blk * MOE_ROWS, dtype=jnp.int32)
    e_r = blk_exp[r // MOE_ROWS]
    off = r - pstarts[e_r]
    valid = (off < counts[e_r]) & (r < pends[-1])
    slot = order[jnp.clip(starts[e_r] + off, 0, n_slot - 1)]
    row_tok = jnp.where(valid, slot // MOE_TOPK, 0)
    row_gate = jnp.where(valid, gate[slot], 0.0).reshape(-1, 1)
    e_s = expert[order]
    pos_sorted = pstarts[e_s] + jnp.arange(n_slot, dtype=jnp.int32) - starts[e_s]
    pos = jnp.zeros((n_slot,), jnp.int32).at[order].set(pos_sorted)
    xs = jnp.take(x, row_tok, axis=0)
    out = _expert_ffn(xs, blk_exp, n_valid, row_gate, wg, wu, wd)
    pos = pos.reshape(n_tok, MOE_TOPK)
    return jnp.take(out, pos[:, 0], axis=0), jnp.take(out, pos[:, 1], axis=0)


def kernel(x_prompt, x_sample, cache_kv_ab, state_kv_c_w128, state_kv_c_w512, state_kv_c_w2048, page_table, rel_bias,
           w_in_ab, w_out_ab, w_in_c, w_out_c, ln_mix_g, ln_mix_b, w_route_grp, b_route_grp, w_route_exp, b_route_exp,
           w_gate, w_up, w_down, ln_ffn_g, ln_ffn_b):
    bsz, seq, dm = x_prompt.shape
    nseq, t_len, _ = x_sample.shape
    past_len = page_table.shape[1] * PAGE_SIZE
    c_states = (state_kv_c_w128, state_kv_c_w512, state_kv_c_w2048)
    xp = x_prompt.reshape(bsz * seq, dm)
    xs = x_sample.reshape(nseq * t_len, dm)
    d_ab = AB_HEADS * HEAD_DIM
    d_c = C_HEADS * HEAD_DIM
    d_cg = C_GROUP_HEADS * HEAD_DIM
    outs = {}

    for layer in range(DEPTH):
        li = layer // 2
        if layer % 2 == 0:
            w_in = w_in_ab[li]
            qp = _matmul(xp, w_in, 0, 1, d_ab)
            kvp = _matmul(xp, w_in, d_ab // 512, 1, 2 * d_ab)
            qs = _matmul(xs, w_in, 0, 1, d_ab)
            kvs = _matmul(xs, w_in, d_ab // 512, 1, 2 * d_ab)
            att_p = _ab_prompt(qp, kvp, _moba_prompt_bias(rel_bias), bsz, seq)
            cache = cache_kv_ab[li].reshape(cache_kv_ab.shape[1], PAGE_SIZE * 2 * AB_HEADS, HEAD_DIM)
            att_s = _ab_sample(qs, kvs, cache, page_table,
                               _moba_sample_bias(rel_bias, page_table.shape[1], t_len, past_len), t_len)
            mix_p = _matmul(att_p, w_out_ab[li], 0, 1, dm)
            mix_s = _matmul(att_s, w_out_ab[li], 0, 1, dm)
            outs["kv_ab_p"] = kvp.reshape(1, bsz, seq, 2, AB_HEADS, HEAD_DIM)
            outs["kv_ab_s"] = kvs.reshape(1, nseq, t_len, 2, AB_HEADS, HEAD_DIM)
        else:
            w_in = w_in_c[li]
            qp = _matmul(xp, w_in, 0, 1, d_c)
            qs = _matmul(xs, w_in, 0, 1, d_c)
            kvp, kvs = [], []
            for g in range(len(C_GROUPS)):
                kvp.append(_matmul(xp, w_in, d_c // 512 + g, d_c // 512, 2 * d_cg))
                kvs.append(_matmul(xs, w_in, d_c // 512 + g, d_c // 512, 2 * d_cg))
            att_p = _c_prompt(qp, kvp, _c_prompt_masks(rel_bias, seq), bsz, seq)
            q_pad = jnp.pad(qs.reshape(nseq * t_len, len(C_GROUPS), C_GROUP_HEADS, HEAD_DIM),
                            ((0, 0), (0, 0), (0, 8 - C_GROUP_HEADS), (0, 0)))
            kvn3 = [kv.reshape(nseq * t_len, 8, HEAD_DIM) for kv in kvs]
            st8 = [st[li][None].reshape(1, nseq, st.shape[2], 8, HEAD_DIM) for st in c_states]
            att_s = _c_sample(q_pad, kvn3, st8, _c_sample_masks(rel_bias), t_len)
            att_s = att_s[:, :, :C_GROUP_HEADS].reshape(nseq * t_len, d_cg)
            mix_p = _matmul(att_p, w_out_c[li], 0, 1, dm)
            mix_s = _matmul(att_s, w_out_c[li], 0, 1, dm)
            for g, (win, _) in enumerate(C_GROUPS):
                keep = min(win, seq)
                kv5 = kvp[g].reshape(bsz, seq, 2, C_GROUP_HEADS, HEAD_DIM)
                outs[f"c_p{g}"] = kv5[:, seq - keep:][None]
                buf = jnp.concatenate([c_states[g][li], kvs[g].reshape(nseq, t_len, 2, C_GROUP_HEADS, HEAD_DIM)], axis=1)
                keep = min(win, buf.shape[1])
                outs[f"c_s{g}"] = buf[:, buf.shape[1] - keep:][None]
        xp = _res_ln(xp, [mix_p], ln_mix_g[layer], ln_mix_b[layer])
        xs = _res_ln(xs, [mix_s], ln_mix_g[layer], ln_mix_b[layer])
        moe_w = (w_route_grp[layer], b_route_grp[layer], w_route_exp[layer], b_route_exp[layer],
                 w_gate[layer], w_up[layer], w_down[layer])
        xp = _res_ln(xp, _hier_moe(xp, *moe_w), ln_ffn_g[layer], ln_ffn_b[layer])
        xs = _res_ln(xs, _hier_moe(xs, *moe_w), ln_ffn_g[layer], ln_ffn_b[layer])

    return (xp.reshape(bsz, seq, dm), xs.reshape(nseq, t_len, dm), outs["kv_ab_p"], outs["kv_ab_s"],
            outs["c_p0"], outs["c_s0"], outs["c_p1"], outs["c_s1"], outs["c_p2"], outs["c_s2"])
```

```python
import functools
import math

import jax
import jax.numpy as jnp
import numpy as np
from jax import lax
from jax.experimental import pallas as pl
from jax.experimental.pallas import tpu as pltpu

F32 = jnp.float32
BF16 = jnp.bfloat16
NEG_INF = float("-inf")
HIGHEST = lax.Precision.HIGHEST

HEAD_DIM = 128
MOBA_HEADS = 12
SB_HEADS = 4
AB_HEADS = MOBA_HEADS + SB_HEADS
MOBA_BLOCK = 256
MOBA_TOPK = 3
PAGE_SIZE = 128
C_GROUPS = ((128, 1), (512, 4), (2048, 16))
C_GROUP_HEADS = 4
C_HEADS = C_GROUP_HEADS * len(C_GROUPS)
N_BUCKETS = 32
REL_MAX_DIST = 128
MOE_TOPK = 2
DEPTH = 2
ALPHA = (2 * DEPTH) ** 0.25
LN_EPS = 1e-5
SCALE = HEAD_DIM ** -0.5

SUBLANES = 8
QBLK = 256
MOE_ROWS = 256
C_SLABS = 144
PAGES_PER_STEP = 4
N_NEAR = 1
VMEM_LIMIT = 56 * 1024 * 1024


def _dot(a, b, **kw):
    return jnp.dot(a, b, preferred_element_type=F32, **kw)


def _dot_nt(a, b, **kw):
    return lax.dot_general(a, b, (((1,), (1,)), ((), ())), preferred_element_type=F32, **kw)


def _split_dot(x, u):
    hi = x.astype(BF16)
    lo = (x - hi.astype(F32)).astype(BF16)
    return _dot(hi, u) + _dot(lo, u)


def _log_sigmoid(z):
    return jnp.minimum(z, 0.0) - jnp.log(1.0 + jnp.exp(-jnp.abs(z)))


def _params(*sem):
    return pltpu.CompilerParams(dimension_semantics=sem, vmem_limit_bytes=VMEM_LIMIT)


def _bucket_np(dist):
    dist = np.asarray(dist, np.int32)
    exact = N_BUCKETS // 2
    far = np.maximum(dist, exact).astype(np.float32)
    frac = np.log(far / np.float32(exact)) / np.float32(math.log(REL_MAX_DIST / exact))
    big = exact + (frac * np.float32(N_BUCKETS - exact)).astype(np.int32)
    return np.where(dist < exact, dist, np.minimum(big, N_BUCKETS - 1)).astype(np.int32)


def _bias_lookup(bias_cols, idx):
    flat = jnp.asarray(idx.reshape(-1))
    onehot = (jnp.arange(N_BUCKETS, dtype=jnp.int32)[:, None] == flat[None, :]).astype(F32)
    return jnp.dot(bias_cols.T, onehot, precision=HIGHEST).reshape((bias_cols.shape[1],) + idx.shape)


def _topk_select(gate, valid, n_cand):
    lane = lax.broadcasted_iota(jnp.int32, gate.shape, 1)
    sels = []
    for n in range(n_cand):
        g_n = gate[:, n:n + 1]
        beats = jnp.where(valid, jnp.where((gate > g_n) | ((gate == g_n) & (lane < n)), 1.0, 0.0), 0.0)
        sels.append(jnp.sum(beats, axis=1, keepdims=True) < MOBA_TOPK)
    return sels


def _mm_kernel(x_ref, w_ref, o_ref):
    o_ref[...] = _dot(x_ref[...].astype(BF16), w_ref[...].astype(BF16))


def _matmul(x, w, col0, cstep, ncols, tn=512):
    m, k = x.shape
    tm = min(m, 1024)
    return pl.pallas_call(
        _mm_kernel,
        grid=(m // tm, ncols // tn),
        in_specs=[pl.BlockSpec((tm, k), lambda i, j: (i, 0)),
                  pl.BlockSpec((k, tn), lambda i, j: (0, col0 + j * cstep))],
        out_specs=pl.BlockSpec((tm, tn), lambda i, j: (i, j)),
        out_shape=jax.ShapeDtypeStruct((m, ncols), F32),
        compiler_params=_params("parallel", "parallel"),
        name="matmul",
    )(x, w)


def _layer_norm(y, g_ref, b_ref):
    mu = jnp.mean(y, axis=-1, keepdims=True)
    yc = y - mu
    var = jnp.mean(yc * yc, axis=-1, keepdims=True)
    return yc * lax.rsqrt(var + LN_EPS) * g_ref[...] + b_ref[...]


def _res_ln_kernel(x_ref, a_ref, g_ref, b_ref, o_ref):
    o_ref[...] = _layer_norm(ALPHA * x_ref[...] + a_ref[...], g_ref, b_ref)


def _res_ln(x, a, g, b):
    m, d = x.shape
    tm = min(m, 512)
    row = pl.BlockSpec((tm, d), lambda i: (i, 0))
    vec = pl.BlockSpec((1, d), lambda i: (0, 0))
    return pl.pallas_call(
        _res_ln_kernel,
        grid=(m // tm,),
        in_specs=[row, row, vec, vec],
        out_specs=row,
        out_shape=jax.ShapeDtypeStruct((m, d), F32),
        compiler_params=_params("parallel"),
        name="res_ln",
    )(x, a, g.reshape(1, d), b.reshape(1, d))


def _ab_prompt_kernel(q_ref, k_ref, v_ref, bias_ref, o_ref, means_ref):
    h = pl.program_id(1)
    i = pl.program_id(2)
    n_blk = k_ref.shape[0] // QBLK
    row = lax.broadcasted_iota(jnp.int32, (QBLK, QBLK), 0)
    col = lax.broadcasted_iota(jnp.int32, (QBLK, QBLK), 1)

    def kv_tile(n):
        start = pl.multiple_of(n * QBLK, QBLK)
        return k_ref[pl.ds(start, QBLK), :].astype(BF16), v_ref[pl.ds(start, QBLK), :].astype(BF16)

    @pl.when(h < MOBA_HEADS)
    def _moba():
        @pl.when(i == 0)
        def _():
            means_ref[...] = jnp.zeros_like(means_ref)
            ksum = jnp.sum(k_ref[...].reshape(n_blk, QBLK, HEAD_DIM), axis=1)
            means_ref[0:n_blk, :] = ksum * (1.0 / QBLK)

        q = q_ref[...]
        gate = _dot_nt(q, means_ref[...], precision=HIGHEST)
        lane = lax.broadcasted_iota(jnp.int32, gate.shape, 1)
        sels = _topk_select(gate, lane < i, n_blk)
        sel = jnp.zeros(gate.shape, F32)
        for n in range(n_blk):
            sel = jnp.where(lane == n, jnp.where(sels[n], 1.0, 0.0), sel)

        qs = (q * SCALE).astype(BF16)
        k_t, v_t = kv_tile(i)
        s = _dot_nt(qs, k_t) + bias_ref[0, 0]
        s = jnp.where(col <= row, s, NEG_INF)
        m = jnp.max(s, axis=1, keepdims=True)
        p = jnp.exp(s - m)
        l = jnp.sum(p, axis=1, keepdims=True)
        acc = _dot(p.astype(BF16), v_t)

        def body(n, carry):
            m, l, acc = carry
            k_t, v_t = kv_tile(n)
            s = _dot_nt(qs, k_t) + bias_ref[0, jnp.minimum(i - n, 2)]
            sel_n = jnp.max(jnp.where(lane == n, sel, 0.0), axis=1, keepdims=True)
            s = jnp.where(sel_n > 0.5, s, NEG_INF)
            m_new = jnp.maximum(m, jnp.max(s, axis=1, keepdims=True))
            a = jnp.exp(m - m_new)
            p = jnp.exp(s - m_new)
            l = a * l + jnp.sum(p, axis=1, keepdims=True)
            acc = a * acc + _dot(p.astype(BF16), v_t)
            return m_new, l, acc

        m, l, acc = lax.fori_loop(0, i, body, (m, l, acc))
        o_ref[...] = acc / l

    @pl.when(h >= MOBA_HEADS)
    def _stick_breaking():
        qs = (q_ref[...] * SCALE).astype(BF16)
        upper = jnp.where(row > col, 1.0, 0.0).astype(BF16)

        def tile(n, valid, carry, acc):
            k_t, v_t = kv_tile(n)
            z = _dot_nt(qs, k_t)
            ls = _log_sigmoid(z)
            lk = ls - z
            if valid is not None:
                lk = jnp.where(valid, lk, 0.0)
            w = jnp.exp(ls + _split_dot(lk, upper) + carry)
            if valid is not None:
                w = jnp.where(valid, w, 0.0)
            acc = acc + _dot(w.astype(BF16), v_t)
            return carry + jnp.sum(lk, axis=1, keepdims=True), acc

        carry, acc = tile(i, col < row, jnp.zeros((QBLK, 1), F32), jnp.zeros((QBLK, HEAD_DIM), F32))
        carry, acc = lax.fori_loop(0, i, lambda j, c: tile(i - 1 - j, None, *c), (carry, acc))
        o_ref[...] = acc


def _ab_prompt(q, kv, bias_tiles, bsz, seq):
    nq = seq // QBLK
    return pl.pallas_call(
        _ab_prompt_kernel,
        grid=(bsz, AB_HEADS, nq),
        in_specs=[pl.BlockSpec((QBLK, HEAD_DIM), lambda b, h, i: (b * nq + i, h)),
                  pl.BlockSpec((seq, HEAD_DIM), lambda b, h, i: (b, h)),
                  pl.BlockSpec((seq, HEAD_DIM), lambda b, h, i: (b, AB_HEADS + h)),
                  pl.BlockSpec((1, 3, QBLK, QBLK), lambda b, h, i: (jnp.minimum(h, MOBA_HEADS - 1), 0, 0, 0))],
        out_specs=pl.BlockSpec((QBLK, HEAD_DIM), lambda b, h, i: (b * nq + i, h)),
        out_shape=jax.ShapeDtypeStruct(q.shape, F32),
        scratch_shapes=[pltpu.VMEM((HEAD_DIM, HEAD_DIM), F32)],
        compiler_params=_params("parallel", "parallel", "arbitrary"),
        name="ab_prompt",
    )(q, kv, kv, bias_tiles)


def _moba_prompt_bias(rel_bias):
    t = np.arange(QBLK)
    dist = np.arange(3)[:, None, None] * QBLK + t[None, :, None] - t[None, None, :]
    return _bias_lookup(rel_bias, _bucket_np(np.maximum(dist, 0)))


def _ab_sample_kernel(pt_ref, q_ref, kvn_ref, *refs):
    del pt_ref
    n_pg = PAGES_PER_STEP
    page_refs = refs[:n_pg]
    tab_ref, tabn_ref, o_ref, qs_sc, qg_sc, part_sc, bsum, kscr, vscr, sb_carry, sb_acc = refs[n_pg:]
    s = pl.program_id(1)
    t_len = q_ref.shape[0]
    pages_per_block = MOBA_BLOCK // PAGE_SIZE
    n_blocks = part_sc.shape[0] - 1
    n_pages = n_blocks * pages_per_block
    n_steps = 1 + n_pages // PAGES_PER_STEP
    rows = SUBLANES * t_len
    page_rows = PAGE_SIZE * SUBLANES
    moba_rows = (rows, (MOBA_HEADS - SUBLANES) * t_len)
    eye_r = lax.broadcasted_iota(jnp.int32, (PAGE_SIZE, PAGE_SIZE), 0)
    eye_c = lax.broadcasted_iota(jnp.int32, (PAGE_SIZE, PAGE_SIZE), 1)
    upper = jnp.where(eye_r > eye_c, 1.0, 0.0).astype(BF16)

    def head_cols(ref, h):
        return ref[:, h * HEAD_DIM:(h + 1) * HEAD_DIM]

    def moba_partial(slot, hg, pieces):
        nr = moba_rows[hg]
        qs = qs_sc[hg, 0:nr]
        scores = [_dot_nt(qs, kx) + table for kx, _, table in pieces]
        m = functools.reduce(jnp.maximum, [jnp.max(sc, axis=1, keepdims=True) for sc in scores])
        es = [jnp.exp(sc - m) for sc in scores]
        part_sc[slot, hg, 0, 0:nr] = jnp.broadcast_to(m, (nr, HEAD_DIM))
        part_sc[slot, hg, 1, 0:nr] = jnp.broadcast_to(sum(jnp.sum(e, axis=1, keepdims=True) for e in es), (nr, HEAD_DIM))
        part_sc[slot, hg, 2, 0:nr] = sum(_dot(e, vx) for e, (_, vx, _) in zip(es, pieces))

    def sb_step(keys, values, causal):
        n_chunk = keys[0].shape[0] // PAGE_SIZE
        zs = []
        for j in range(SB_HEADS):
            z = _dot_nt(head_cols(q_ref, MOBA_HEADS + j) * SCALE, keys[j])
            zs += [z[:, c * PAGE_SIZE:(c + 1) * PAGE_SIZE] for c in range(n_chunk)]
        z = jnp.concatenate(zs, axis=0)
        ls = _log_sigmoid(z)
        lk = ls - z
        if causal:
            tq = lax.broadcasted_iota(jnp.int32, z.shape, 0) % t_len
            valid = lax.broadcasted_iota(jnp.int32, z.shape, 1) < tq
            lk = jnp.where(valid, lk, 0.0)
        inner = ls + _split_dot(lk, upper)
        total = jnp.sum(lk, axis=1, keepdims=True)
        for j in range(SB_HEADS):
            later = sb_carry[j]
            ws = [None] * n_chunk
            for c in reversed(range(n_chunk)):
                r0 = (j * n_chunk + c) * t_len
                w = jnp.exp(inner[r0:r0 + t_len] + later)
                ws[c] = jnp.where(valid[r0:r0 + t_len], w, 0.0) if causal else w
                later = later + total[r0:r0 + t_len]
            sb_carry[j] = later
            sb_acc[j] = sb_acc[j] + _dot(jnp.concatenate(ws, axis=1), values[j])

    @pl.when(s == 0)
    def _new_tokens():
        bsum[...] = jnp.zeros_like(bsum)
        sb_carry[...] = jnp.zeros_like(sb_carry)
        sb_acc[...] = jnp.zeros_like(sb_acc)
        pad = jnp.zeros((HEAD_DIM - rows, HEAD_DIM), F32)
        for hg in range(2):
            heads = range(hg * SUBLANES, (hg + 1) * SUBLANES)
            qst = jnp.concatenate([head_cols(q_ref, h) for h in heads], axis=0)
            qg_sc[hg] = qst
            qs_sc[hg] = qst * SCALE
            kx = jnp.concatenate([head_cols(kvn_ref, h) for h in heads] + [pad], axis=0)
            vx = jnp.concatenate([head_cols(kvn_ref, AB_HEADS + h) for h in heads] + [pad], axis=0)
            moba_partial(0, hg, [(kx, vx, tabn_ref[hg, 0:moba_rows[hg]])])
        pad = jnp.zeros((PAGE_SIZE - t_len, HEAD_DIM), F32)
        sb_step([jnp.concatenate([head_cols(kvn_ref, MOBA_HEADS + j), pad], axis=0) for j in range(SB_HEADS)],
                [jnp.concatenate([head_cols(kvn_ref, AB_HEADS + MOBA_HEADS + j), pad], axis=0) for j in range(SB_HEADS)],
                True)

    @pl.when(s > 0)
    def _pages():
        page0 = n_pages - 1 - (s - 1) * PAGES_PER_STEP
        for ub in range(0, PAGES_PER_STEP, pages_per_block):
            blk = (page0 - ub) // pages_per_block
            for hg in range(2):
                pieces = []
                for u in reversed(range(ub, ub + pages_per_block)):
                    near = n_pages - (page0 - u)
                    tix = jnp.where(near <= N_NEAR, near, 0)
                    k3 = page_refs[u][0, :, hg]
                    bsum[hg, blk] = bsum[hg, blk] + jnp.sum(k3, axis=0)
                    kx = k3.reshape(page_rows, HEAD_DIM)
                    vx = page_refs[u][0, :, 2 + hg].reshape(page_rows, HEAD_DIM)
                    pieces.append((kx, vx, tab_ref[tix, hg, 0:moba_rows[hg]]))
                    if hg == 1:
                        kscr[u * page_rows:(u + 1) * page_rows] = kx
                        vscr[u * page_rows:(u + 1) * page_rows] = vx
                moba_partial(blk + 1, hg, pieces)

        def sb_rows(scr, j):
            h8 = MOBA_HEADS - SUBLANES + j
            return jnp.concatenate([scr[pl.ds(u * page_rows + h8, PAGE_SIZE, stride=SUBLANES), :]
                                    for u in reversed(range(PAGES_PER_STEP))], axis=0)

        sb_step([sb_rows(kscr, j) for j in range(SB_HEADS)], [sb_rows(vscr, j) for j in range(SB_HEADS)], False)

    @pl.when(s == n_steps - 1)
    def _finish():
        fold = jnp.where(eye_r // SUBLANES == eye_c, 1.0, 0.0)
        for hg in range(2):
            nr = moba_rows[hg]
            lane = lax.broadcasted_iota(jnp.int32, (nr, HEAD_DIM), 1)
            rowi = lax.broadcasted_iota(jnp.int32, (nr, HEAD_DIM), 0)
            g_all = _dot_nt(qg_sc[hg, 0:nr], bsum[hg].reshape(HEAD_DIM, HEAD_DIM), precision=HIGHEST)
            g_own = jnp.where(lane % SUBLANES == rowi // t_len, g_all, 0.0)
            gate = _dot(g_own, fold, precision=HIGHEST) * (1.0 / MOBA_BLOCK)
            sels = _topk_select(gate, lane < n_blocks, n_blocks)

            def part(slot, k, hg=hg, nr=nr):
                return part_sc[slot, hg, k, 0:nr]

            m_all = part(0, 0)
            for n in range(n_blocks):
                m_all = jnp.maximum(m_all, jnp.where(sels[n], part(n + 1, 0), NEG_INF))
            wgt = jnp.exp(part(0, 0) - m_all)
            num = wgt * part(0, 2)
            den = wgt * part(0, 1)
            for n in range(n_blocks):
                wgt = jnp.where(sels[n], jnp.exp(part(n + 1, 0) - m_all), 0.0)
                num = num + wgt * part(n + 1, 2)
                den = den + wgt * part(n + 1, 1)
            out = num / den
            for h8 in range(nr // t_len):
                h = hg * SUBLANES + h8
                o_ref[:, h * HEAD_DIM:(h + 1) * HEAD_DIM] = out[h8 * t_len:(h8 + 1) * t_len]
        for j in range(SB_HEADS):
            o_ref[:, (MOBA_HEADS + j) * HEAD_DIM:(MOBA_HEADS + j + 1) * HEAD_DIM] = sb_acc[j]


def _ab_sample(q, kvn, cache, page_table, tab, tabn, t_len):
    assert AB_HEADS == 2 * SUBLANES and SUBLANES <= MOBA_HEADS and MOBA_BLOCK % PAGE_SIZE == 0
    nseq, n_pages = page_table.shape
    assert n_pages % PAGES_PER_STEP == 0 and PAGES_PER_STEP * PAGE_SIZE % MOBA_BLOCK == 0
    n_blocks = n_pages * PAGE_SIZE // MOBA_BLOCK
    n_steps = 1 + n_pages // PAGES_PER_STEP
    d_all = AB_HEADS * HEAD_DIM
    rows = SUBLANES * t_len

    def page_map(b, s, pt, *, u):
        page = n_pages - 1 - ((jnp.maximum(s, 1) - 1) * PAGES_PER_STEP + u)
        return (pt[b * n_pages + page], 0, 0, 0, 0)

    page_specs = [pl.BlockSpec((1, PAGE_SIZE, 4, SUBLANES, HEAD_DIM), functools.partial(page_map, u=u))
                  for u in range(PAGES_PER_STEP)]
    grid_spec = pltpu.PrefetchScalarGridSpec(
        num_scalar_prefetch=1,
        grid=(nseq, n_steps),
        in_specs=[pl.BlockSpec((t_len, d_all), lambda b, s, pt: (b, 0)),
                  pl.BlockSpec((t_len, 2 * d_all), lambda b, s, pt: (b, 0))] + page_specs +
                 [pl.BlockSpec(tab.shape, lambda b, s, pt: (0, 0, 0, 0)),
                  pl.BlockSpec(tabn.shape, lambda b, s, pt: (0, 0, 0))],
        out_specs=pl.BlockSpec((t_len, d_all), lambda b, s, pt: (b, 0)),
        scratch_shapes=[pltpu.VMEM((2, rows, HEAD_DIM), F32),
                        pltpu.VMEM((2, rows, HEAD_DIM), F32),
                        pltpu.VMEM((n_blocks + 1, 2, 3, rows, HEAD_DIM), F32),
                        pltpu.VMEM((2, HEAD_DIM // SUBLANES, SUBLANES, HEAD_DIM), F32),
                        pltpu.VMEM((PAGES_PER_STEP * PAGE_SIZE * SUBLANES, HEAD_DIM), F32),
                        pltpu.VMEM((PAGES_PER_STEP * PAGE_SIZE * SUBLANES, HEAD_DIM), F32),
                        pltpu.VMEM((SB_HEADS, t_len, HEAD_DIM), F32),
                        pltpu.VMEM((SB_HEADS, t_len, HEAD_DIM), F32)],
    )
    return pl.pallas_call(
        _ab_sample_kernel,
        grid_spec=grid_spec,
        out_shape=jax.ShapeDtypeStruct((nseq * t_len, d_all), F32),
        compiler_params=_params("parallel", "arbitrary"),
        name="ab_sample",
    )(page_table.reshape(-1), q, kvn, *([cache] * PAGES_PER_STEP), tab, tabn)


def _moba_sample_tables(rel_bias, n_pages, t_len, past_len):
    assert N_NEAR * PAGE_SIZE + 1 >= REL_MAX_DIST and past_len == n_pages * PAGE_SIZE
    t = np.arange(t_len)
    pos = np.arange(PAGE_SIZE)
    dist = [np.full((t_len, PAGE_SIZE), REL_MAX_DIST)]
    for near in range(1, N_NEAR + 1):
        dist.append(past_len + t[:, None] - ((n_pages - near) * PAGE_SIZE + pos[None, :]))
    idx = _bucket_np(np.maximum(np.stack(dist), 0))
    n_tab = idx.shape[0]
    same = np.eye(SUBLANES, dtype=bool)
    pad_heads = ((0, AB_HEADS - MOBA_HEADS),)
    bias = jnp.pad(_bias_lookup(rel_bias, idx), pad_heads + ((0, 0),) * 3)
    bias = bias.reshape(2, SUBLANES, n_tab, t_len, PAGE_SIZE).transpose(2, 0, 1, 3, 4)
    tab = jnp.where(same[None, None, :, None, None, :], bias[..., None], NEG_INF)
    tab = tab.reshape(n_tab, 2, SUBLANES * t_len, PAGE_SIZE * SUBLANES)
    idx_n = _bucket_np(np.maximum(t[:, None] - t[None, :], 0))
    bias_n = jnp.pad(_bias_lookup(rel_bias, idx_n), pad_heads + ((0, 0),) * 2).reshape(2, SUBLANES, t_len, t_len)
    ok = same[:, None, :, None] & (t[None, :, None, None] >= t[None, None, None, :])
    tabn = jnp.where(ok[None], bias_n[:, :, :, None, :], NEG_INF).reshape(2, SUBLANES * t_len, SUBLANES * t_len)
    tabn = jnp.pad(tabn, ((0, 0), (0, 0), (0, HEAD_DIM - SUBLANES * t_len)), constant_values=NEG_INF)
    return tab, tabn


def _c_tile_counts(seq):
    return tuple(min(win // QBLK + 1, seq // QBLK) if win >= QBLK else 2 for win, _ in C_GROUPS)


def _c_prompt_kernel(counts, *refs):
    q_refs, kv_refs, (mask_ref, o_ref) = refs[0:3], refs[3:9], refs[9:]
    i = pl.program_id(2)
    state = None
    tile0 = 0
    for g, n_tiles in enumerate(counts):
        qs = (q_refs[g][...] * SCALE).astype(BF16)
        k_ref, v_ref = kv_refs[2 * g], kv_refs[2 * g + 1]

        def step(off, carry, qs=qs, k_ref=k_ref, v_ref=v_ref, tile0=tile0):
            start = pl.multiple_of((i - off) * QBLK, QBLK)
            k_t = k_ref[pl.ds(start, QBLK), :].astype(BF16)
            v_t = v_ref[pl.ds(start, QBLK), :].astype(BF16)
            sc = _dot_nt(qs, k_t) + mask_ref[0, tile0 + off]
            if carry is None:
                m_new = jnp.max(sc, axis=1, keepdims=True)
                p = jnp.exp(sc - m_new)
                return m_new, jnp.sum(p, axis=1, keepdims=True), _dot(p.astype(BF16), v_t)
            m, l, acc = carry
            m_new = jnp.maximum(m, jnp.max(sc, axis=1, keepdims=True))
            a = jnp.exp(m - m_new)
            p = jnp.exp(sc - m_new)
            return m_new, a * l + jnp.sum(p, axis=1, keepdims=True), a * acc + _dot(p.astype(BF16), v_t)

        first = 0
        if state is None:
            state = step(0, None)
            first = 1
        state = lax.fori_loop(first, jnp.minimum(i, n_tiles - 1) + 1, step, state)
        tile0 += n_tiles
    m, l, acc = state
    o_ref[...] = acc / l


def _c_prompt(q, kvs, mask_tiles, bsz, seq):
    nq = seq // QBLK
    counts = _c_tile_counts(seq)
    q_specs = [pl.BlockSpec((QBLK, HEAD_DIM), lambda b, j, i, g=g: (b * nq + i, C_GROUP_HEADS * g + j))
               for g in range(len(C_GROUPS))]
    kv_specs = []
    for g in range(len(C_GROUPS)):
        kv_specs.append(pl.BlockSpec((seq, HEAD_DIM), lambda b, j, i: (b, j)))
        kv_specs.append(pl.BlockSpec((seq, HEAD_DIM), lambda b, j, i: (b, C_GROUP_HEADS + j)))
    kv_args = [kvs[g] for g in range(len(C_GROUPS)) for _ in range(2)]
    return pl.pallas_call(
        functools.partial(_c_prompt_kernel, counts),
        grid=(bsz, C_GROUP_HEADS, nq),
        in_specs=q_specs + kv_specs + [pl.BlockSpec((1, sum(counts), QBLK, QBLK), lambda b, j, i: (j, 0, 0, 0))],
        out_specs=pl.BlockSpec((QBLK, HEAD_DIM), lambda b, j, i: (b * nq + i, j)),
        out_shape=jax.ShapeDtypeStruct((bsz * seq, C_GROUP_HEADS * HEAD_DIM), F32),
        compiler_params=_params("parallel", "parallel", "arbitrary"),
        name="c_prompt",
    )(q, q, q, *kv_args, mask_tiles)


def _c_prompt_masks(rel_bias, seq):
    t = np.arange(QBLK)
    tiles = []
    for g, ((win, dil), n_tiles) in enumerate(zip(C_GROUPS, _c_tile_counts(seq))):
        d = np.arange(n_tiles)[:, None, None] * QBLK + t[None, :, None] - t[None, None, :]
        ok = (d >= 0) & (d <= win) & (d % dil == 0)
        bias = _bias_lookup(rel_bias[:, C_GROUP_HEADS * g:C_GROUP_HEADS * (g + 1)], _bucket_np(np.maximum(d, 0)))
        tiles.append(jnp.where(ok[None], bias, NEG_INF))
    return jnp.concatenate(tiles, axis=1)


def _c_sample_kernel(q_ref, *refs):
    n_g = len(C_GROUPS)
    kvn_refs, st_refs = refs[0:n_g], refs[n_g:2 * n_g]
    mask_ref, o_ref = refs[2 * n_g], refs[2 * n_g + 1]
    cats = refs[2 * n_g + 2:]
    t_len = q_ref.shape[0]
    for g, (win, dil) in enumerate(C_GROUPS):
        n_buf = st_refs[g].shape[2]
        cat = cats[g]
        cat[0:n_buf] = st_refs[g][0, 0]
        cat[n_buf:n_buf + t_len] = kvn_refs[g][...]
        cat[n_buf + t_len:] = jnp.zeros((cat.shape[0] - n_buf - t_len,) + cat.shape[1:], F32)
    for t in range(t_len):
        xs, scores = [], []
        for g, (win, dil) in enumerate(C_GROUPS):
            n_buf = st_refs[g].shape[2]
            m_max = win // dil
            x = cats[g][pl.ds(n_buf + t - m_max * dil, C_SLABS, stride=dil)].reshape(C_SLABS * 8, HEAD_DIM)
            xs.append(x)
            scores.append(_dot_nt(q_ref[t, g] * SCALE, x) + mask_ref[g])
        m = functools.reduce(jnp.maximum, [jnp.max(sc, axis=1, keepdims=True) for sc in scores])
        ps = [jnp.exp(sc - m) for sc in scores]
        l = sum(jnp.sum(p, axis=1, keepdims=True) for p in ps)
        acc = sum(_dot(pltpu.roll(p, C_GROUP_HEADS, axis=1), x) for p, x in zip(ps, xs))
        o_ref[0, t] = acc / l


def _c_sample(q, kvns, states, masks, t_len):
    nseq = states[0].shape[1]
    n_g = len(C_GROUPS)
    in_specs = [pl.BlockSpec((t_len, n_g, 8, HEAD_DIM), lambda b: (b, 0, 0, 0))]
    in_specs += [pl.BlockSpec((t_len, 8, HEAD_DIM), lambda b: (b, 0, 0)) for _ in range(n_g)]
    in_specs += [pl.BlockSpec((1, 1) + st.shape[2:], lambda b: (0, b, 0, 0, 0)) for st in states]
    in_specs += [pl.BlockSpec(masks.shape, lambda b: (0, 0, 0))]
    scratch = []
    for st, (win, dil) in zip(states, C_GROUPS):
        n_buf = st.shape[2]
        assert n_buf == win
        rows = n_buf - win + (C_SLABS - 1) * dil + t_len
        scratch.append(pltpu.VMEM((max(rows, n_buf + t_len), 8, HEAD_DIM), F32))
    return pl.pallas_call(
        _c_sample_kernel,
        grid=(nseq,),
        in_specs=in_specs,
        out_specs=pl.BlockSpec((1, t_len, 8, HEAD_DIM), lambda b: (b, 0, 0, 0)),
        out_shape=jax.ShapeDtypeStruct((nseq, t_len, 8, HEAD_DIM), F32),
        scratch_shapes=scratch,
        compiler_params=_params("parallel"),
        name="c_sample",
    )(q, *kvns, *states, masks)


def _c_sample_masks(rel_bias):
    c = np.arange(C_SLABS * 8)
    slab, is_v, head = c // 8, (c % 8) // C_GROUP_HEADS, c % C_GROUP_HEADS
    slot = np.arange(8)
    out = []
    for g, (win, dil) in enumerate(C_GROUPS):
        m_max = win // dil
        steps = m_max - slab
        ok = (steps >= 0)[None, :] & (is_v == 0)[None, :] & (head[None, :] == slot[:, None])
        bias = _bias_lookup(rel_bias[:, C_GROUP_HEADS * g:C_GROUP_HEADS * (g + 1)],
                            _bucket_np(np.maximum(steps, 0) * dil))
        bias = jnp.concatenate([bias, jnp.zeros_like(bias)], axis=0)
        out.append(jnp.where(slot[:, None] < C_GROUP_HEADS, jnp.where(ok, bias, NEG_INF), 0.0))
    return jnp.stack(out)


def _router_kernel(n_grp, epg, x_ref, w_ref, b_ref, e_ref, g_ref):
    logits = _dot(x_ref[...], w_ref[...], precision=HIGHEST) + b_ref[...]
    lane = lax.broadcasted_iota(jnp.int32, logits.shape, 1)
    big = jnp.int32(2 ** 30)

    def first_max(valid):
        v = jnp.max(jnp.where(valid, logits, NEG_INF), axis=1, keepdims=True)
        idx = jnp.min(jnp.where(valid & (logits == v), lane, big), axis=1, keepdims=True)
        return v, idx

    in_grp = lane < n_grp
    g_max, grp = first_max(in_grp)
    p_grp = 1.0 / jnp.sum(jnp.where(in_grp, jnp.exp(logits - g_max), 0.0), axis=1, keepdims=True)
    lo = n_grp + grp * epg
    in_exp = (lane >= lo) & (lane < lo + epg)
    v1, i1 = first_max(in_exp)
    v2, i2 = first_max(in_exp & (lane != i1))
    e2 = jnp.exp(v2 - v1)
    g1 = p_grp / (1.0 + e2)
    g2 = p_grp * e2 / (1.0 + e2)
    e_ref[...] = jnp.where(lane == 0, i1 - n_grp, jnp.where(lane == 1, i2 - n_grp, 0))
    g_ref[...] = jnp.where(lane == 0, g1, jnp.where(lane == 1, g2, 0.0))


def _router(x, w, b, n_grp, epg):
    n_tok, dm = x.shape
    tm = min(n_tok, 512)
    row = pl.BlockSpec((tm, HEAD_DIM), lambda i: (i, 0))
    return pl.pallas_call(
        functools.partial(_router_kernel, n_grp, epg),
        grid=(n_tok // tm,),
        in_specs=[pl.BlockSpec((tm, dm), lambda i: (i, 0)),
                  pl.BlockSpec((dm, HEAD_DIM), lambda i: (0, 0)),
                  pl.BlockSpec((1, HEAD_DIM), lambda i: (0, 0))],
        out_specs=[row, row],
        out_shape=[jax.ShapeDtypeStruct((n_tok, HEAD_DIM), jnp.int32), jax.ShapeDtypeStruct((n_tok, HEAD_DIM), F32)],
        compiler_params=_params("parallel"),
        name="router",
    )(x, w, b)


def _rank_kernel(e_ref, r_ref, c_ref, carry):
    @pl.when(pl.program_id(0) == 0)
    def _():
        carry[...] = jnp.zeros_like(carry)

    e = e_ref[...]
    tm = e.shape[0]
    lane = lax.broadcasted_iota(jnp.int32, e.shape, 1)
    hits = [lane == e[:, k:k + 1] for k in range(MOE_TOPK)]
    both = sum(jnp.where(hit, 1.0, 0.0) for hit in hits)
    earlier = jnp.where(lax.broadcasted_iota(jnp.int32, (tm, tm), 0) > lax.broadcasted_iota(jnp.int32, (tm, tm), 1),
                        1.0, 0.0).astype(BF16)
    prefix = _dot(earlier, both.astype(BF16)) + carry[0:1, :]
    rank = jnp.zeros(e.shape, F32)
    for k, hit in enumerate(hits):
        r_k = jnp.sum(jnp.where(hit, prefix, 0.0), axis=1, keepdims=True)
        if k:
            r_k = r_k + sum(jnp.where(e[:, k:k + 1] == e[:, kk:kk + 1], 1.0, 0.0) for kk in range(k))
        rank = jnp.where(lane == k, r_k, rank)
    r_ref[...] = rank.astype(jnp.int32)
    carry[...] = carry[...] + jnp.sum(both, axis=0, keepdims=True)
    c_ref[...] = carry[...]


def _rank(eid):
    n_tok = eid.shape[0]
    tm = 512
    row = pl.BlockSpec((tm, HEAD_DIM), lambda i: (i, 0))
    return pl.pallas_call(
        _rank_kernel,
        grid=(n_tok // tm,),
        in_specs=[row],
        out_specs=[row, pl.BlockSpec((SUBLANES, HEAD_DIM), lambda i: (0, 0))],
        out_shape=[jax.ShapeDtypeStruct((n_tok, HEAD_DIM), jnp.int32), jax.ShapeDtypeStruct((SUBLANES, HEAD_DIM), F32)],
        scratch_shapes=[pltpu.VMEM((SUBLANES, HEAD_DIM), F32)],
        compiler_params=_params("arbitrary"),
        name="moe_rank",
    )(eid)


def _row_copies(pos_ref, slot0, tm, make_copy):
    def copy(r, k):
        return make_copy(r, k, pos_ref[slot0 + r * MOE_TOPK + k])

    def issue(r, c):
        for k in range(MOE_TOPK):
            copy(r, k).start()
        return c

    def drain(r, c):
        for k in range(MOE_TOPK):
            copy(r, k).wait()
        return c

    lax.fori_loop(0, tm, issue, 0)
    lax.fori_loop(0, tm, drain, 0)


def _dispatch_kernel(tok0, pos_ref, x_ref, rows_in, rows_out, sem):
    del rows_in
    tm = x_ref.shape[0]
    slot0 = (tok0 + pl.program_id(0) * tm) * MOE_TOPK
    _row_copies(pos_ref, slot0, tm, lambda r, k, dst: pltpu.make_async_copy(
        x_ref.at[pl.ds(r, 1), :], rows_out.at[pl.ds(dst, 1), :], sem))


def _dispatch(pos, x, rows, tok0):
    n_tok, dm = x.shape
    tm = min(n_tok, 512)
    grid_spec = pltpu.PrefetchScalarGridSpec(
        num_scalar_prefetch=1,
        grid=(n_tok // tm,),
        in_specs=[pl.BlockSpec((tm, dm), lambda i, pos: (i, 0)), pl.BlockSpec(memory_space=pl.ANY)],
        out_specs=pl.BlockSpec(memory_space=pl.ANY),
        scratch_shapes=[pltpu.SemaphoreType.DMA(())],
    )
    return pl.pallas_call(
        functools.partial(_dispatch_kernel, tok0),
        grid_spec=grid_spec,
        out_shape=jax.ShapeDtypeStruct(rows.shape, F32),
        input_output_aliases={2: 0},
        compiler_params=_params("arbitrary"),
        name="moe_dispatch",
    )(pos, x, rows)


def _ffn_kernel(be_ref, nv_ref, x_ref, wg_ref, wu_ref, wd_ref, o_ref):
    del be_ref

    @pl.when(pl.program_id(0) < nv_ref[0])
    def _():
        xb = x_ref[...].astype(BF16)
        g = _dot(xb, wg_ref[0].astype(BF16))
        u = _dot(xb, wu_ref[0].astype(BF16))
        hid = g * (1.0 / (1.0 + jnp.exp(-g))) * u
        o_ref[...] = _dot(hid.astype(BF16), wd_ref[0].astype(BF16))

    @pl.when(pl.program_id(0) >= nv_ref[0])
    def _():
        o_ref[...] = jnp.zeros_like(o_ref)


def _expert_ffn(rows, blk_exp, n_valid, wg, wu, wd):
    n_rows, dm = rows.shape
    hid = wg.shape[2]
    grid_spec = pltpu.PrefetchScalarGridSpec(
        num_scalar_prefetch=2,
        grid=(n_rows // MOE_ROWS,),
        in_specs=[pl.BlockSpec((MOE_ROWS, dm), lambda i, be, nv: (i, 0)),
                  pl.BlockSpec((1, dm, hid), lambda i, be, nv: (be[i], 0, 0)),
                  pl.BlockSpec((1, dm, hid), lambda i, be, nv: (be[i], 0, 0)),
                  pl.BlockSpec((1, hid, dm), lambda i, be, nv: (be[i], 0, 0))],
        out_specs=pl.BlockSpec((MOE_ROWS, dm), lambda i, be, nv: (i, 0)),
    )
    return pl.pallas_call(
        _ffn_kernel,
        grid_spec=grid_spec,
        out_shape=jax.ShapeDtypeStruct((n_rows, dm), F32),
        compiler_params=_params("arbitrary"),
        name="expert_ffn",
    )(blk_exp, n_valid, rows, wg, wu, wd)


def _combine_ln_kernel(tok0, pos_ref, x_ref, gate_ref, rows_ref, g_ref, b_ref, o_ref, buf, sem):
    tm = x_ref.shape[0]
    slot0 = (tok0 + pl.program_id(0) * tm) * MOE_TOPK
    _row_copies(pos_ref, slot0, tm, lambda r, k, src: pltpu.make_async_copy(
        rows_ref.at[pl.ds(src, 1), :], buf.at[k, pl.ds(r, 1), :], sem))
    gates = gate_ref[...]
    y = ALPHA * x_ref[...]
    for k in range(MOE_TOPK):
        y = y + gates[:, k:k + 1] * buf[k]
    o_ref[...] = _layer_norm(y, g_ref, b_ref)


def _combine_ln(pos, x, gates, rows, g, b, tok0):
    n_tok, dm = x.shape
    tm = 256
    row = pl.BlockSpec((tm, dm), lambda i, pos: (i, 0))
    vec = pl.BlockSpec((1, dm), lambda i, pos: (0, 0))
    grid_spec = pltpu.PrefetchScalarGridSpec(
        num_scalar_prefetch=1,
        grid=(n_tok // tm,),
        in_specs=[row, pl.BlockSpec((tm, HEAD_DIM), lambda i, pos: (i, 0)), pl.BlockSpec(memory_space=pl.ANY), vec, vec],
        out_specs=row,
        scratch_shapes=[pltpu.VMEM((MOE_TOPK, tm, dm), F32), pltpu.SemaphoreType.DMA(())],
    )
    return pl.pallas_call(
        functools.partial(_combine_ln_kernel, tok0),
        grid_spec=grid_spec,
        out_shape=jax.ShapeDtypeStruct((n_tok, dm), F32),
        compiler_params=_params("arbitrary"),
        name="moe_combine_ln",
    )(pos, x, gates, rows, g.reshape(1, dm), b.reshape(1, dm))


def _moe_ln(xp, xs, w_rg, b_rg, w_re, b_re, wg, wu, wd, ln_g, ln_b):
    dm = xp.shape[1]
    n_grp, _, epg = w_re.shape
    n_exp = wg.shape[0]
    n_col = n_grp + n_grp * epg
    w = jnp.concatenate([w_rg, w_re.transpose(1, 0, 2).reshape(dm, n_grp * epg)], axis=1)
    w = jnp.pad(w, ((0, 0), (0, HEAD_DIM - n_col)))
    b = jnp.pad(jnp.concatenate([b_rg, b_re.reshape(-1)]), (0, HEAD_DIM - n_col)).reshape(1, HEAD_DIM)
    eid_p, gate_p = _router(xp, w, b, n_grp, epg)
    eid_s, gate_s = _router(xs, w, b, n_grp, epg)
    eid = jnp.concatenate([eid_p, eid_s], axis=0)
    rank, counts = _rank(eid)
    counts = counts[0, :n_exp].astype(jnp.int32)
    pcounts = (counts + MOE_ROWS - 1) // MOE_ROWS * MOE_ROWS
    pends = jnp.cumsum(pcounts)
    pstarts = pends - pcounts
    choice = eid[:, :MOE_TOPK]
    start_of = jnp.sum(jnp.where(choice[..., None] == jnp.arange(n_exp, dtype=jnp.int32), pstarts, 0), axis=-1)
    pos = (rank[:, :MOE_TOPK] + start_of).reshape(-1)
    n_slot = pos.shape[0]
    n_blk = -(-(n_slot + n_exp * (MOE_ROWS - 1)) // MOE_ROWS)
    blk_start = jnp.arange(n_blk, dtype=jnp.int32) * MOE_ROWS
    blk_exp = jnp.minimum(jnp.sum(blk_start[:, None] >= pends[None, :], axis=1), n_exp - 1).astype(jnp.int32)
    n_valid = (pends[-1] // MOE_ROWS).astype(jnp.int32).reshape(1)
    rows = jnp.zeros((n_blk * MOE_ROWS, dm), F32)
    rows = _dispatch(pos, xp, rows, 0)
    rows = _dispatch(pos, xs, rows, xp.shape[0])
    rows = _expert_ffn(rows, blk_exp, n_valid, wg, wu, wd)
    return (_combine_ln(pos, xp, gate_p, rows, ln_g, ln_b, 0),
            _combine_ln(pos, xs, gate_s, rows, ln_g, ln_b, xp.shape[0]))


def kernel(x_prompt, x_sample, cache_kv_ab, state_kv_c_w128, state_kv_c_w512, state_kv_c_w2048, page_table, rel_bias,
           w_in_ab, w_out_ab, w_in_c, w_out_c, ln_mix_g, ln_mix_b, w_route_grp, b_route_grp, w_route_exp, b_route_exp,
           w_gate, w_up, w_down, ln_ffn_g, ln_ffn_b):
    bsz, seq, dm = x_prompt.shape
    nseq, t_len, _ = x_sample.shape
    n_pages = page_table.shape[1]
    past_len = n_pages * PAGE_SIZE
    c_states = (state_kv_c_w128, state_kv_c_w512, state_kv_c_w2048)
    xp = x_prompt.reshape(bsz * seq, dm)
    xs = x_sample.reshape(nseq * t_len, dm)
    d_ab = AB_HEADS * HEAD_DIM
    d_c = C_HEADS * HEAD_DIM
    d_cg = C_GROUP_HEADS * HEAD_DIM
    outs = {}

    for layer in range(DEPTH):
        li = layer // 2
        if layer % 2 == 0:
            w_in = w_in_ab[li]
            qp = _matmul(xp, w_in, 0, 1, d_ab)
            kvp = _matmul(xp, w_in, d_ab // 512, 1, 2 * d_ab)
            qs = _matmul(xs, w_in, 0, 1, d_ab)
            kvs = _matmul(xs, w_in, d_ab // 512, 1, 2 * d_ab)
            att_p = _ab_prompt(qp, kvp, _moba_prompt_bias(rel_bias), bsz, seq)
            cache = cache_kv_ab[li].reshape(cache_kv_ab.shape[1], PAGE_SIZE, 4, SUBLANES, HEAD_DIM)
            tab, tabn = _moba_sample_tables(rel_bias, n_pages, t_len, past_len)
            att_s = _ab_sample(qs, kvs, cache, page_table, tab, tabn, t_len)
            mix_p = _matmul(att_p, w_out_ab[li], 0, 1, dm)
            mix_s = _matmul(att_s, w_out_ab[li], 0, 1, dm)
            outs["kv_ab_p"] = kvp.reshape(1, bsz, seq, 2, AB_HEADS, HEAD_DIM)
            outs["kv_ab_s"] = kvs.reshape(1, nseq, t_len, 2, AB_HEADS, HEAD_DIM)
        else:
            w_in = w_in_c[li]
            qp = _matmul(xp, w_in, 0, 1, d_c)
            qs = _matmul(xs, w_in, 0, 1, d_c)
            kvp, kvs = [], []
            for g in range(len(C_GROUPS)):
                kvp.append(_matmul(xp, w_in, d_c // 512 + g, d_c // 512, 2 * d_cg))
                kvs.append(_matmul(xs, w_in, d_c // 512 + g, d_c // 512, 2 * d_cg))
            att_p = _c_prompt(qp, kvp, _c_prompt_masks(rel_bias, seq), bsz, seq)
            q_pad = jnp.pad(qs.reshape(nseq * t_len, len(C_GROUPS), C_GROUP_HEADS, HEAD_DIM),
                            ((0, 0), (0, 0), (0, 8 - C_GROUP_HEADS), (0, 0)))
            kvn3 = [kv.reshape(nseq * t_len, 8, HEAD_DIM) for kv in kvs]
            st8 = [st[li][None].reshape(1, nseq, st.shape[2], 8, HEAD_DIM) for st in c_states]
            att_s = _c_sample(q_pad, kvn3, st8, _c_sample_masks(rel_bias), t_len)
            att_s = att_s[:, :, :C_GROUP_HEADS].reshape(nseq * t_len, d_cg)
            mix_p = _matmul(att_p, w_out_c[li], 0, 1, dm)
            mix_s = _matmul(att_s, w_out_c[li], 0, 1, dm)
            for g, (win, _) in enumerate(C_GROUPS):
                keep = min(win, seq)
                kv5 = kvp[g].reshape(bsz, seq, 2, C_GROUP_HEADS, HEAD_DIM)
                outs[f"c_p{g}"] = kv5[:, seq - keep:][None]
                buf = jnp.concatenate([c_states[g][li], kvs[g].reshape(nseq, t_len, 2, C_GROUP_HEADS, HEAD_DIM)], axis=1)
                keep = min(win, buf.shape[1])
                outs[f"c_s{g}"] = buf[:, buf.shape[1] - keep:][None]
        xp = _res_ln(xp, mix_p, ln_mix_g[layer], ln_mix_b[layer])
        xs = _res_ln(xs, mix_s, ln_mix_g[layer], ln_mix_b[layer])
        xp, xs = _moe_ln(xp, xs, w_route_grp[layer], b_route_grp[layer], w_route_exp[layer], b_route_exp[layer],
                         w_gate[layer], w_up[layer], w_down[layer], ln_ffn_g[layer], ln_ffn_b[layer])

    return (xp.reshape(bsz, seq, dm), xs.reshape(nseq, t_len, dm), outs["kv_ab_p"], outs["kv_ab_s"],
            outs["c_p0"], outs["c_s0"], outs["c_p1"], outs["c_s1"], outs["c_p2"], outs["c_s2"])
```

```python
import functools
import math

import jax
import jax.numpy as jnp
import numpy as np
from jax import lax
from jax.experimental import pallas as pl
from jax.experimental.pallas import tpu as pltpu

F32 = jnp.float32
BF16 = jnp.bfloat16
NEG_INF = float("-inf")
HIGHEST = lax.Precision.HIGHEST

HEAD_DIM = 128
MOBA_HEADS = 12
SB_HEADS = 4
AB_HEADS = MOBA_HEADS + SB_HEADS
MOBA_BLOCK = 256
MOBA_TOPK = 3
PAGE_SIZE = 128
C_GROUPS = ((128, 1), (512, 4), (2048, 16))
C_GROUP_HEADS = 4
C_HEADS = C_GROUP_HEADS * len(C_GROUPS)
N_BUCKETS = 32
REL_MAX_DIST = 128
MOE_TOPK = 2
DEPTH = 2
ALPHA = (2 * DEPTH) ** 0.25
LN_EPS = 1e-5
SCALE = HEAD_DIM ** -0.5

SUBLANES = 8
QBLK = 256
MOE_ROWS = 256
C_SLABS = 144
PAGES_PER_STEP = 4
N_NEAR = 1
VMEM_LIMIT = 56 * 1024 * 1024


def _dot(a, b, **kw):
    return jnp.dot(a, b, preferred_element_type=F32, **kw)


def _dot_nt(a, b, **kw):
    return lax.dot_general(a, b, (((1,), (1,)), ((), ())), preferred_element_type=F32, **kw)


def _split_dot(x, u):
    hi = x.astype(BF16)
    lo = (x - hi.astype(F32)).astype(BF16)
    return _dot(hi, u) + _dot(lo, u)


def _log_sigmoid(z):
    return jnp.minimum(z, 0.0) - jnp.log(1.0 + jnp.exp(-jnp.abs(z)))


def _params(*sem):
    return pltpu.CompilerParams(dimension_semantics=sem, vmem_limit_bytes=VMEM_LIMIT)


def _bucket_np(dist):
    dist = np.asarray(dist, np.int32)
    exact = N_BUCKETS // 2
    far = np.maximum(dist, exact).astype(np.float32)
    frac = np.log(far / np.float32(exact)) / np.float32(math.log(REL_MAX_DIST / exact))
    big = exact + (frac * np.float32(N_BUCKETS - exact)).astype(np.int32)
    return np.where(dist < exact, dist, np.minimum(big, N_BUCKETS - 1)).astype(np.int32)


def _bias_lookup(bias_cols, idx):
    flat = jnp.asarray(idx.reshape(-1))
    onehot = (jnp.arange(N_BUCKETS, dtype=jnp.int32)[:, None] == flat[None, :]).astype(F32)
    return jnp.dot(bias_cols.T, onehot, precision=HIGHEST).reshape((bias_cols.shape[1],) + idx.shape)


def _topk_select(gate, valid, n_cand):
    lane = lax.broadcasted_iota(jnp.int32, gate.shape, 1)
    sels = []
    for n in range(n_cand):
        g_n = gate[:, n:n + 1]
        beats = jnp.where(valid, jnp.where((gate > g_n) | ((gate == g_n) & (lane < n)), 1.0, 0.0), 0.0)
        sels.append(jnp.sum(beats, axis=1, keepdims=True) < MOBA_TOPK)
    return sels


def _mm_kernel(x_ref, w_ref, o_ref):
    o_ref[...] = _dot(x_ref[...].astype(BF16), w_ref[...].astype(BF16))


def _matmul(x, w, col0, cstep, ncols, tn=512):
    m, k = x.shape
    tm = min(m, 1024)
    return pl.pallas_call(
        _mm_kernel,
        grid=(m // tm, ncols // tn),
        in_specs=[pl.BlockSpec((tm, k), lambda i, j: (i, 0)),
                  pl.BlockSpec((k, tn), lambda i, j: (0, col0 + j * cstep))],
        out_specs=pl.BlockSpec((tm, tn), lambda i, j: (i, j)),
        out_shape=jax.ShapeDtypeStruct((m, ncols), F32),
        compiler_params=_params("parallel", "parallel"),
        name="matmul",
    )(x, w)


def _layer_norm(y, g_ref, b_ref):
    mu = jnp.mean(y, axis=-1, keepdims=True)
    yc = y - mu
    var = jnp.mean(yc * yc, axis=-1, keepdims=True)
    return yc * lax.rsqrt(var + LN_EPS) * g_ref[...] + b_ref[...]


def _res_ln_kernel(x_ref, a_ref, g_ref, b_ref, o_ref):
    o_ref[...] = _layer_norm(ALPHA * x_ref[...] + a_ref[...], g_ref, b_ref)


def _res_ln(x, a, g, b):
    m, d = x.shape
    tm = min(m, 512)
    row = pl.BlockSpec((tm, d), lambda i: (i, 0))
    vec = pl.BlockSpec((1, d), lambda i: (0, 0))
    return pl.pallas_call(
        _res_ln_kernel,
        grid=(m // tm,),
        in_specs=[row, row, vec, vec],
        out_specs=row,
        out_shape=jax.ShapeDtypeStruct((m, d), F32),
        compiler_params=_params("parallel"),
        name="res_ln",
    )(x, a, g.reshape(1, d), b.reshape(1, d))


def _ab_prompt_kernel(q_ref, k_ref, v_ref, bias_ref, o_ref, means_ref, sel_sc):
    h = pl.program_id(1)
    i = pl.program_id(2)
    n_blk = k_ref.shape[0] // QBLK
    row = lax.broadcasted_iota(jnp.int32, (QBLK, QBLK), 0)
    col = lax.broadcasted_iota(jnp.int32, (QBLK, QBLK), 1)
    n_steps = i // 2 + 1

    def kv_pair(jp):
        start = pl.multiple_of(jp * 2 * QBLK, 2 * QBLK)
        return k_ref[pl.ds(start, 2 * QBLK), :].astype(BF16), v_ref[pl.ds(start, 2 * QBLK), :].astype(BF16)

    def halves(x):
        return x[:, 0:QBLK], x[:, QBLK:2 * QBLK]

    @pl.when(h < MOBA_HEADS)
    def _moba():
        @pl.when(i == 0)
        def _():
            means_ref[...] = jnp.sum(k_ref[...].reshape(n_blk, QBLK, HEAD_DIM), axis=1) * (1.0 / QBLK)

        q = q_ref[...]
        gate = _dot_nt(means_ref[...], q, precision=HIGHEST)
        blk = lax.broadcasted_iota(jnp.int32, gate.shape, 0)
        sel_t = jnp.zeros(gate.shape, F32)
        for n in range(n_blk):
            g_n = gate[n:n + 1, :]
            beats = jnp.where(blk < i, jnp.where((gate > g_n) | ((gate == g_n) & (blk < n)), 1.0, 0.0), 0.0)
            top = jnp.where(jnp.sum(beats, axis=0, keepdims=True) < MOBA_TOPK, 1.0, 0.0) * jnp.where(n < i, 1.0, 0.0)
            sel_t = jnp.where(blk == n, top, sel_t)
        sel_t = jnp.concatenate([sel_t, jnp.zeros((HEAD_DIM - n_blk, QBLK), F32)], axis=0).astype(BF16)
        sel = _dot_nt(jnp.where(row == col, 1.0, 0.0).astype(BF16), sel_t)
        for n in range(n_blk):
            sel_sc[n] = jnp.broadcast_to(sel[:, n:n + 1], (QBLK, HEAD_DIM))

        qs = (q * SCALE).astype(BF16)

        def body(jj, carry):
            m, l, acc = carry
            jp = i // 2 - jj
            k_t, v_t = kv_pair(jp)
            parts = []
            for half, s in enumerate(halves(_dot_nt(qs, k_t))):
                nb = 2 * jp + half
                off = i - nb
                chosen = jnp.concatenate([sel_sc[nb]] * (QBLK // HEAD_DIM), axis=1)
                keep = jnp.where(off == 0, jnp.where(col <= row, 1.0, 0.0), chosen) > 0.5
                parts.append(jnp.where(keep, s + bias_ref[0, jnp.clip(off, 0, 2)], NEG_INF))
            s = jnp.concatenate(parts, axis=1)
            m_new = jnp.maximum(m, jnp.max(s, axis=1, keepdims=True))
            a = jnp.exp(m - m_new)
            p = jnp.exp(s - m_new)
            return m_new, a * l + jnp.sum(p, axis=1, keepdims=True), a * acc + _dot(p.astype(BF16), v_t)

        init = (jnp.full((QBLK, 1), NEG_INF, F32), jnp.zeros((QBLK, 1), F32), jnp.zeros((QBLK, HEAD_DIM), F32))
        m, l, acc = lax.fori_loop(0, n_steps, body, init)
        o_ref[...] = acc / l

    @pl.when(h >= MOBA_HEADS)
    def _stick_breaking():
        qs = (q_ref[...] * SCALE).astype(BF16)
        upper = jnp.where(row > col, 1.0, 0.0).astype(BF16)

        def body(jj, carry):
            later, acc = carry
            jp = i // 2 - jj
            k_t, v_t = kv_pair(jp)
            z = _dot_nt(qs, k_t)
            ls = _log_sigmoid(z)
            lk = ls - z
            keeps, lks = [], []
            for half, lk_h in enumerate(halves(lk)):
                off = i - (2 * jp + half)
                keeps.append(jnp.where(off == 0, jnp.where(col < row, 1.0, 0.0), jnp.where(off > 0, 1.0, 0.0)) > 0.5)
                lks.append(jnp.where(keeps[half], lk_h, 0.0))
            inner = _split_dot(jnp.concatenate(lks, axis=0), upper)
            total = [jnp.sum(x, axis=1, keepdims=True) for x in lks]
            after = (inner[0:QBLK] + (later + total[1]), inner[QBLK:2 * QBLK] + later)
            ws = [jnp.where(keeps[half], jnp.exp(ls_h + after[half]), 0.0) for half, ls_h in enumerate(halves(ls))]
            acc = acc + _dot(jnp.concatenate(ws, axis=1).astype(BF16), v_t)
            return later + total[0] + total[1], acc

        init = (jnp.zeros((QBLK, 1), F32), jnp.zeros((QBLK, HEAD_DIM), F32))
        o_ref[...] = lax.fori_loop(0, n_steps, body, init)[1]


def _ab_prompt(q, kv, bias_tiles, bsz, seq):
    nq = seq // QBLK
    return pl.pallas_call(
        _ab_prompt_kernel,
        grid=(bsz, AB_HEADS, nq),
        in_specs=[pl.BlockSpec((QBLK, HEAD_DIM), lambda b, h, i: (b * nq + i, h)),
                  pl.BlockSpec((seq, HEAD_DIM), lambda b, h, i: (b, h)),
                  pl.BlockSpec((seq, HEAD_DIM), lambda b, h, i: (b, AB_HEADS + h)),
                  pl.BlockSpec((1, 3, QBLK, QBLK), lambda b, h, i: (jnp.minimum(h, MOBA_HEADS - 1), 0, 0, 0))],
        out_specs=pl.BlockSpec((QBLK, HEAD_DIM), lambda b, h, i: (b * nq + i, h)),
        out_shape=jax.ShapeDtypeStruct(q.shape, F32),
        scratch_shapes=[pltpu.VMEM((seq // QBLK, HEAD_DIM), F32),
                        pltpu.VMEM((seq // QBLK, QBLK, HEAD_DIM), F32)],
        compiler_params=_params("parallel", "parallel", "arbitrary"),
        name="ab_prompt",
    )(q, kv, kv, bias_tiles)


def _moba_prompt_bias(rel_bias):
    t = np.arange(QBLK)
    dist = np.arange(3)[:, None, None] * QBLK + t[None, :, None] - t[None, None, :]
    return _bias_lookup(rel_bias, _bucket_np(np.maximum(dist, 0)))


def _ab_sample_kernel(pt_ref, q_ref, kvn_ref, *refs):
    del pt_ref
    n_pg = PAGES_PER_STEP
    page_refs = refs[:n_pg]
    tab_ref, tabn_ref, o_ref, qs_sc, qg_sc, part_sc, bsum, kscr, vscr, sb_carry, sb_acc = refs[n_pg:]
    s = pl.program_id(1)
    t_len = q_ref.shape[0]
    pages_per_block = MOBA_BLOCK // PAGE_SIZE
    n_blocks = part_sc.shape[0] - 1
    n_pages = n_blocks * pages_per_block
    n_steps = 1 + n_pages // PAGES_PER_STEP
    rows = SUBLANES * t_len
    page_rows = PAGE_SIZE * SUBLANES
    moba_rows = (rows, (MOBA_HEADS - SUBLANES) * t_len)
    eye_r = lax.broadcasted_iota(jnp.int32, (PAGE_SIZE, PAGE_SIZE), 0)
    eye_c = lax.broadcasted_iota(jnp.int32, (PAGE_SIZE, PAGE_SIZE), 1)
    upper = jnp.where(eye_r > eye_c, 1.0, 0.0).astype(BF16)

    def head_cols(ref, h):
        return ref[:, h * HEAD_DIM:(h + 1) * HEAD_DIM]

    def moba_partial(slot, hg, pieces):
        nr = moba_rows[hg]
        qs = qs_sc[hg, 0:nr]
        scores = [_dot_nt(qs, kx) + table for kx, _, table in pieces]
        m = functools.reduce(jnp.maximum, [jnp.max(sc, axis=1, keepdims=True) for sc in scores])
        es = [jnp.exp(sc - m) for sc in scores]
        part_sc[slot, hg, 0, 0:nr] = jnp.broadcast_to(m, (nr, HEAD_DIM))
        part_sc[slot, hg, 1, 0:nr] = jnp.broadcast_to(sum(jnp.sum(e, axis=1, keepdims=True) for e in es), (nr, HEAD_DIM))
        part_sc[slot, hg, 2, 0:nr] = sum(_dot(e, vx) for e, (_, vx, _) in zip(es, pieces))

    def sb_step(keys, values, causal):
        n_chunk = keys[0].shape[0] // PAGE_SIZE
        zs = []
        for j in range(SB_HEADS):
            z = _dot_nt(head_cols(q_ref, MOBA_HEADS + j) * SCALE, keys[j])
            zs += [z[:, c * PAGE_SIZE:(c + 1) * PAGE_SIZE] for c in range(n_chunk)]
        z = jnp.concatenate(zs, axis=0)
        ls = _log_sigmoid(z)
        lk = ls - z
        if causal:
            tq = lax.broadcasted_iota(jnp.int32, z.shape, 0) % t_len
            valid = lax.broadcasted_iota(jnp.int32, z.shape, 1) < tq
            lk = jnp.where(valid, lk, 0.0)
        inner = ls + _split_dot(lk, upper)
        total = jnp.sum(lk, axis=1, keepdims=True)
        for j in range(SB_HEADS):
            later = sb_carry[j]
            ws = [None] * n_chunk
            for c in reversed(range(n_chunk)):
                r0 = (j * n_chunk + c) * t_len
                w = jnp.exp(inner[r0:r0 + t_len] + later)
                ws[c] = jnp.where(valid[r0:r0 + t_len], w, 0.0) if causal else w
                later = later + total[r0:r0 + t_len]
            sb_carry[j] = later
            sb_acc[j] = sb_acc[j] + _dot(jnp.concatenate(ws, axis=1), values[j])

    @pl.when(s == 0)
    def _new_tokens():
        bsum[...] = jnp.zeros_like(bsum)
        sb_carry[...] = jnp.zeros_like(sb_carry)
        sb_acc[...] = jnp.zeros_like(sb_acc)
        pad = jnp.zeros((HEAD_DIM - rows, HEAD_DIM), F32)
        for hg in range(2):
            heads = range(hg * SUBLANES, (hg + 1) * SUBLANES)
            qst = jnp.concatenate([head_cols(q_ref, h) for h in heads], axis=0)
            qg_sc[hg] = qst
            qs_sc[hg] = qst * SCALE
            kx = jnp.concatenate([head_cols(kvn_ref, h) for h in heads] + [pad], axis=0)
            vx = jnp.concatenate([head_cols(kvn_ref, AB_HEADS + h) for h in heads] + [pad], axis=0)
            moba_partial(0, hg, [(kx, vx, tabn_ref[hg, 0:moba_rows[hg]])])
        pad = jnp.zeros((PAGE_SIZE - t_len, HEAD_DIM), F32)
        sb_step([jnp.concatenate([head_cols(kvn_ref, MOBA_HEADS + j), pad], axis=0) for j in range(SB_HEADS)],
                [jnp.concatenate([head_cols(kvn_ref, AB_HEADS + MOBA_HEADS + j), pad], axis=0) for j in range(SB_HEADS)],
                True)

    @pl.when(s > 0)
    def _pages():
        page0 = n_pages - 1 - (s - 1) * PAGES_PER_STEP
        for ub in range(0, PAGES_PER_STEP, pages_per_block):
            blk = (page0 - ub) // pages_per_block
            for hg in range(2):
                pieces = []
                for u in reversed(range(ub, ub + pages_per_block)):
                    near = n_pages - (page0 - u)
                    tix = jnp.where(near <= N_NEAR, near, 0)
                    k3 = page_refs[u][0, :, hg]
                    bsum[hg, blk] = bsum[hg, blk] + jnp.sum(k3, axis=0)
                    kx = k3.reshape(page_rows, HEAD_DIM)
                    vx = page_refs[u][0, :, 2 + hg].reshape(page_rows, HEAD_DIM)
                    pieces.append((kx, vx, tab_ref[tix, hg, 0:moba_rows[hg]]))
                    if hg == 1:
                        kscr[u * page_rows:(u + 1) * page_rows] = kx
                        vscr[u * page_rows:(u + 1) * page_rows] = vx
                moba_partial(blk + 1, hg, pieces)

        def sb_rows(scr, j):
            h8 = MOBA_HEADS - SUBLANES + j
            return jnp.concatenate([scr[pl.ds(u * page_rows + h8, PAGE_SIZE, stride=SUBLANES), :]
                                    for u in reversed(range(PAGES_PER_STEP))], axis=0)

        sb_step([sb_rows(kscr, j) for j in range(SB_HEADS)], [sb_rows(vscr, j) for j in range(SB_HEADS)], False)

    @pl.when(s == n_steps - 1)
    def _finish():
        fold = jnp.where(eye_r // SUBLANES == eye_c, 1.0, 0.0)
        for hg in range(2):
            nr = moba_rows[hg]
            lane = lax.broadcasted_iota(jnp.int32, (nr, HEAD_DIM), 1)
            rowi = lax.broadcasted_iota(jnp.int32, (nr, HEAD_DIM), 0)
            g_all = _dot_nt(qg_sc[hg, 0:nr], bsum[hg].reshape(HEAD_DIM, HEAD_DIM), precision=HIGHEST)
            g_own = jnp.where(lane % SUBLANES == rowi // t_len, g_all, 0.0)
            gate = _dot(g_own, fold, precision=HIGHEST) * (1.0 / MOBA_BLOCK)
            sels = _topk_select(gate, lane < n_blocks, n_blocks)

            def part(slot, k, hg=hg, nr=nr):
                return part_sc[slot, hg, k, 0:nr]

            m_all = part(0, 0)
            for n in range(n_blocks):
                m_all = jnp.maximum(m_all, jnp.where(sels[n], part(n + 1, 0), NEG_INF))
            wgt = jnp.exp(part(0, 0) - m_all)
            num = wgt * part(0, 2)
            den = wgt * part(0, 1)
            for n in range(n_blocks):
                wgt = jnp.where(sels[n], jnp.exp(part(n + 1, 0) - m_all), 0.0)
                num = num + wgt * part(n + 1, 2)
                den = den + wgt * part(n + 1, 1)
            out = num / den
            for h8 in range(nr // t_len):
                h = hg * SUBLANES + h8
                o_ref[:, h * HEAD_DIM:(h + 1) * HEAD_DIM] = out[h8 * t_len:(h8 + 1) * t_len]
        for j in range(SB_HEADS):
            o_ref[:, (MOBA_HEADS + j) * HEAD_DIM:(MOBA_HEADS + j + 1) * HEAD_DIM] = sb_acc[j]


def _ab_sample(q, kvn, cache, page_table, tab, tabn, t_len):
    assert AB_HEADS == 2 * SUBLANES and SUBLANES <= MOBA_HEADS and MOBA_BLOCK % PAGE_SIZE == 0
    nseq, n_pages = page_table.shape
    assert n_pages % PAGES_PER_STEP == 0 and PAGES_PER_STEP * PAGE_SIZE % MOBA_BLOCK == 0
    n_blocks = n_pages * PAGE_SIZE // MOBA_BLOCK
    n_steps = 1 + n_pages // PAGES_PER_STEP
    d_all = AB_HEADS * HEAD_DIM
    rows = SUBLANES * t_len

    def page_map(b, s, pt, *, u):
        page = n_pages - 1 - ((jnp.maximum(s, 1) - 1) * PAGES_PER_STEP + u)
        return (pt[b * n_pages + page], 0, 0, 0, 0)

    page_specs = [pl.BlockSpec((1, PAGE_SIZE, 4, SUBLANES, HEAD_DIM), functools.partial(page_map, u=u))
                  for u in range(PAGES_PER_STEP)]
    grid_spec = pltpu.PrefetchScalarGridSpec(
        num_scalar_prefetch=1,
        grid=(nseq, n_steps),
        in_specs=[pl.BlockSpec((t_len, d_all), lambda b, s, pt: (b, 0)),
                  pl.BlockSpec((t_len, 2 * d_all), lambda b, s, pt: (b, 0))] + page_specs +
                 [pl.BlockSpec(tab.shape, lambda b, s, pt: (0, 0, 0, 0)),
                  pl.BlockSpec(tabn.shape, lambda b, s, pt: (0, 0, 0))],
        out_specs=pl.BlockSpec((t_len, d_all), lambda b, s, pt: (b, 0)),
        scratch_shapes=[pltpu.VMEM((2, rows, HEAD_DIM), F32),
                        pltpu.VMEM((2, rows, HEAD_DIM), F32),
                        pltpu.VMEM((n_blocks + 1, 2, 3, rows, HEAD_DIM), F32),
                        pltpu.VMEM((2, HEAD_DIM // SUBLANES, SUBLANES, HEAD_DIM), F32),
                        pltpu.VMEM((PAGES_PER_STEP * PAGE_SIZE * SUBLANES, HEAD_DIM), F32),
                        pltpu.VMEM((PAGES_PER_STEP * PAGE_SIZE * SUBLANES, HEAD_DIM), F32),
                        pltpu.VMEM((SB_HEADS, t_len, HEAD_DIM), F32),
                        pltpu.VMEM((SB_HEADS, t_len, HEAD_DIM), F32)],
    )
    return pl.pallas_call(
        _ab_sample_kernel,
        grid_spec=grid_spec,
        out_shape=jax.ShapeDtypeStruct((nseq * t_len, d_all), F32),
        compiler_params=_params("parallel", "arbitrary"),
        name="ab_sample",
    )(page_table.reshape(-1), q, kvn, *([cache] * PAGES_PER_STEP), tab, tabn)


def _moba_sample_tables(rel_bias, n_pages, t_len, past_len):
    assert N_NEAR * PAGE_SIZE + 1 >= REL_MAX_DIST and past_len == n_pages * PAGE_SIZE
    t = np.arange(t_len)
    pos = np.arange(PAGE_SIZE)
    dist = [np.full((t_len, PAGE_SIZE), REL_MAX_DIST)]
    for near in range(1, N_NEAR + 1):
        dist.append(past_len + t[:, None] - ((n_pages - near) * PAGE_SIZE + pos[None, :]))
    idx = _bucket_np(np.maximum(np.stack(dist), 0))
    n_tab = idx.shape[0]
    same = np.eye(SUBLANES, dtype=bool)
    pad_heads = ((0, AB_HEADS - MOBA_HEADS),)
    bias = jnp.pad(_bias_lookup(rel_bias, idx), pad_heads + ((0, 0),) * 3)
    bias = bias.reshape(2, SUBLANES, n_tab, t_len, PAGE_SIZE).transpose(2, 0, 1, 3, 4)
    tab = jnp.where(same[None, None, :, None, None, :], bias[..., None], NEG_INF)
    tab = tab.reshape(n_tab, 2, SUBLANES * t_len, PAGE_SIZE * SUBLANES)
    idx_n = _bucket_np(np.maximum(t[:, None] - t[None, :], 0))
    bias_n = jnp.pad(_bias_lookup(rel_bias, idx_n), pad_heads + ((0, 0),) * 2).reshape(2, SUBLANES, t_len, t_len)
    ok = same[:, None, :, None] & (t[None, :, None, None] >= t[None, None, None, :])
    tabn = jnp.where(ok[None], bias_n[:, :, :, None, :], NEG_INF).reshape(2, SUBLANES * t_len, SUBLANES * t_len)
    tabn = jnp.pad(tabn, ((0, 0), (0, 0), (0, HEAD_DIM - SUBLANES * t_len)), constant_values=NEG_INF)
    return tab, tabn


def _c_tile_counts(seq):
    return tuple(min(win // QBLK + 1, seq // QBLK) if win >= QBLK else 2 for win, _ in C_GROUPS)


def _c_prompt_kernel(counts, *refs):
    q_refs, kv_refs, (mask_ref, o_ref) = refs[0:3], refs[3:9], refs[9:]
    i = pl.program_id(2)
    state = None
    tile0 = 0
    for g, n_tiles in enumerate(counts):
        qs = (q_refs[g][...] * SCALE).astype(BF16)
        k_ref, v_ref = kv_refs[2 * g], kv_refs[2 * g + 1]

        def step(off, carry, qs=qs, k_ref=k_ref, v_ref=v_ref, tile0=tile0):
            start = pl.multiple_of((i - off) * QBLK, QBLK)
            k_t = k_ref[pl.ds(start, QBLK), :].astype(BF16)
            v_t = v_ref[pl.ds(start, QBLK), :].astype(BF16)
            sc = _dot_nt(qs, k_t) + mask_ref[0, tile0 + off]
            if carry is None:
                m_new = jnp.max(sc, axis=1, keepdims=True)
                p = jnp.exp(sc - m_new)
                return m_new, jnp.sum(p, axis=1, keepdims=True), _dot(p.astype(BF16), v_t)
            m, l, acc = carry
            m_new = jnp.maximum(m, jnp.max(sc, axis=1, keepdims=True))
            a = jnp.exp(m - m_new)
            p = jnp.exp(sc - m_new)
            return m_new, a * l + jnp.sum(p, axis=1, keepdims=True), a * acc + _dot(p.astype(BF16), v_t)

        first = 0
        if state is None:
            state = step(0, None)
            first = 1
        state = lax.fori_loop(first, jnp.minimum(i, n_tiles - 1) + 1, step, state)
        tile0 += n_tiles
    m, l, acc = state
    o_ref[...] = acc / l


def _c_prompt(q, kvs, mask_tiles, bsz, seq):
    nq = seq // QBLK
    counts = _c_tile_counts(seq)
    q_specs = [pl.BlockSpec((QBLK, HEAD_DIM), lambda b, j, i, g=g: (b * nq + i, C_GROUP_HEADS * g + j))
               for g in range(len(C_GROUPS))]
    kv_specs = []
    for g in range(len(C_GROUPS)):
        kv_specs.append(pl.BlockSpec((seq, HEAD_DIM), lambda b, j, i: (b, j)))
        kv_specs.append(pl.BlockSpec((seq, HEAD_DIM), lambda b, j, i: (b, C_GROUP_HEADS + j)))
    kv_args = [kvs[g] for g in range(len(C_GROUPS)) for _ in range(2)]
    return pl.pallas_call(
        functools.partial(_c_prompt_kernel, counts),
        grid=(bsz, C_GROUP_HEADS, nq),
        in_specs=q_specs + kv_specs + [pl.BlockSpec((1, sum(counts), QBLK, QBLK), lambda b, j, i: (j, 0, 0, 0))],
        out_specs=pl.BlockSpec((QBLK, HEAD_DIM), lambda b, j, i: (b * nq + i, j)),
        out_shape=jax.ShapeDtypeStruct((bsz * seq, C_GROUP_HEADS * HEAD_DIM), F32),
        compiler_params=_params("parallel", "parallel", "arbitrary"),
        name="c_prompt",
    )(q, q, q, *kv_args, mask_tiles)


def _c_prompt_masks(rel_bias, seq):
    t = np.arange(QBLK)
    tiles = []
    for g, ((win, dil), n_tiles) in enumerate(zip(C_GROUPS, _c_tile_counts(seq))):
        d = np.arange(n_tiles)[:, None, None] * QBLK + t[None, :, None] - t[None, None, :]
        ok = (d >= 0) & (d <= win) & (d % dil == 0)
        bias = _bias_lookup(rel_bias[:, C_GROUP_HEADS * g:C_GROUP_HEADS * (g + 1)], _bucket_np(np.maximum(d, 0)))
        tiles.append(jnp.where(ok[None], bias, NEG_INF))
    return jnp.concatenate(tiles, axis=1)


def _c_sample_kernel(q_ref, *refs):
    n_g = len(C_GROUPS)
    kvn_refs, st_refs = refs[0:n_g], refs[n_g:2 * n_g]
    mask_ref, o_ref = refs[2 * n_g], refs[2 * n_g + 1]
    next_refs = refs[2 * n_g + 2:]
    t_len = q_ref.shape[0]
    for g in range(n_g):
        n_buf = st_refs[g].shape[2]
        next_refs[g][0, 0, 0:n_buf - t_len] = st_refs[g][0, 0, t_len:n_buf]
        next_refs[g][0, 0, n_buf - t_len:n_buf] = kvn_refs[g][...]
    for t in range(t_len):
        xs, scores = [], []
        for g, (win, dil) in enumerate(C_GROUPS):
            n_buf = st_refs[g].shape[2]
            m_max = win // dil
            first = n_buf + t - m_max * dil
            n_old = -(-(n_buf - first) // dil)
            old = st_refs[g][0, 0, pl.ds(first, n_old, stride=dil)]
            new = [kvn_refs[g][first + c * dil - n_buf][None] for c in range(n_old, m_max + 1)]
            pad = jnp.zeros((C_SLABS - m_max - 1, 8, HEAD_DIM), F32)
            x = jnp.concatenate([old] + new + [pad], axis=0).reshape(C_SLABS * 8, HEAD_DIM)
            xs.append(x)
            scores.append(_dot_nt(q_ref[t, g] * SCALE, x) + mask_ref[g])
        m = functools.reduce(jnp.maximum, [jnp.max(sc, axis=1, keepdims=True) for sc in scores])
        ps = [jnp.exp(sc - m) for sc in scores]
        l = sum(jnp.sum(p, axis=1, keepdims=True) for p in ps)
        acc = sum(_dot(pltpu.roll(p, C_GROUP_HEADS, axis=1), x) for p, x in zip(ps, xs))
        o_ref[0, t] = acc / l


def _c_sample(q, kvns, states, masks, t_len):
    nseq = states[0].shape[1]
    n_g = len(C_GROUPS)
    in_specs = [pl.BlockSpec((t_len, n_g, 8, HEAD_DIM), lambda b: (b, 0, 0, 0))]
    in_specs += [pl.BlockSpec((t_len, 8, HEAD_DIM), lambda b: (b, 0, 0)) for _ in range(n_g)]
    in_specs += [pl.BlockSpec((1, 1) + st.shape[2:], lambda b: (0, b, 0, 0, 0)) for st in states]
    in_specs += [pl.BlockSpec(masks.shape, lambda b: (0, 0, 0))]
    for st, (win, dil) in zip(states, C_GROUPS):
        assert st.shape[2] == win and win // dil < C_SLABS
    st_specs = [pl.BlockSpec((1, 1) + st.shape[2:], lambda b: (0, b, 0, 0, 0)) for st in states]
    return pl.pallas_call(
        _c_sample_kernel,
        grid=(nseq,),
        in_specs=in_specs,
        out_specs=[pl.BlockSpec((1, t_len, 8, HEAD_DIM), lambda b: (b, 0, 0, 0))] + st_specs,
        out_shape=[jax.ShapeDtypeStruct((nseq, t_len, 8, HEAD_DIM), F32)] +
                  [jax.ShapeDtypeStruct(st.shape, F32) for st in states],
        compiler_params=_params("parallel"),
        name="c_sample",
    )(q, *kvns, *states, masks)


def _c_sample_masks(rel_bias):
    c = np.arange(C_SLABS * 8)
    slab, is_v, head = c // 8, (c % 8) // C_GROUP_HEADS, c % C_GROUP_HEADS
    slot = np.arange(8)
    out = []
    for g, (win, dil) in enumerate(C_GROUPS):
        m_max = win // dil
        steps = m_max - slab
        ok = (steps >= 0)[None, :] & (is_v == 0)[None, :] & (head[None, :] == slot[:, None])
        bias = _bias_lookup(rel_bias[:, C_GROUP_HEADS * g:C_GROUP_HEADS * (g + 1)],
                            _bucket_np(np.maximum(steps, 0) * dil))
        bias = jnp.concatenate([bias, jnp.zeros_like(bias)], axis=0)
        out.append(jnp.where(slot[:, None] < C_GROUP_HEADS, jnp.where(ok, bias, NEG_INF), 0.0))
    return jnp.stack(out)


def _router_kernel(n_grp, epg, x_ref, w_ref, b_ref, e_ref, g_ref):
    logits = _dot(x_ref[...], w_ref[...], precision=HIGHEST) + b_ref[...]
    lane = lax.broadcasted_iota(jnp.int32, logits.shape, 1)
    big = jnp.int32(2 ** 30)

    def first_max(valid):
        v = jnp.max(jnp.where(valid, logits, NEG_INF), axis=1, keepdims=True)
        idx = jnp.min(jnp.where(valid & (logits == v), lane, big), axis=1, keepdims=True)
        return v, idx

    in_grp = lane < n_grp
    g_max, grp = first_max(in_grp)
    p_grp = 1.0 / jnp.sum(jnp.where(in_grp, jnp.exp(logits - g_max), 0.0), axis=1, keepdims=True)
    lo = n_grp + grp * epg
    in_exp = (lane >= lo) & (lane < lo + epg)
    v1, i1 = first_max(in_exp)
    v2, i2 = first_max(in_exp & (lane != i1))
    e2 = jnp.exp(v2 - v1)
    g1 = p_grp / (1.0 + e2)
    g2 = p_grp * e2 / (1.0 + e2)
    e_ref[...] = jnp.where(lane == 0, i1 - n_grp, jnp.where(lane == 1, i2 - n_grp, 0))
    g_ref[...] = jnp.where(lane == 0, g1, jnp.where(lane == 1, g2, 0.0))


def _router(x, w, b, n_grp, epg):
    n_tok, dm = x.shape
    tm = min(n_tok, 512)
    row = pl.BlockSpec((tm, HEAD_DIM), lambda i: (i, 0))
    return pl.pallas_call(
        functools.partial(_router_kernel, n_grp, epg),
        grid=(n_tok // tm,),
        in_specs=[pl.BlockSpec((tm, dm), lambda i: (i, 0)),
                  pl.BlockSpec((dm, HEAD_DIM), lambda i: (0, 0)),
                  pl.BlockSpec((1, HEAD_DIM), lambda i: (0, 0))],
        out_specs=[row, row],
        out_shape=[jax.ShapeDtypeStruct((n_tok, HEAD_DIM), jnp.int32), jax.ShapeDtypeStruct((n_tok, HEAD_DIM), F32)],
        compiler_params=_params("parallel"),
        name="router",
    )(x, w, b)


def _rank_kernel(e_ref, r_ref, c_ref, carry):
    @pl.when(pl.program_id(0) == 0)
    def _():
        carry[...] = jnp.zeros_like(carry)

    e = e_ref[...]
    tm = e.shape[0]
    lane = lax.broadcasted_iota(jnp.int32, e.shape, 1)
    hits = [lane == e[:, k:k + 1] for k in range(MOE_TOPK)]
    both = sum(jnp.where(hit, 1.0, 0.0) for hit in hits)
    earlier = jnp.where(lax.broadcasted_iota(jnp.int32, (tm, tm), 0) > lax.broadcasted_iota(jnp.int32, (tm, tm), 1),
                        1.0, 0.0).astype(BF16)
    prefix = _dot(earlier, both.astype(BF16)) + carry[0:1, :]
    rank = jnp.zeros(e.shape, F32)
    for k, hit in enumerate(hits):
        r_k = jnp.sum(jnp.where(hit, prefix, 0.0), axis=1, keepdims=True)
        if k:
            r_k = r_k + sum(jnp.where(e[:, k:k + 1] == e[:, kk:kk + 1], 1.0, 0.0) for kk in range(k))
        rank = jnp.where(lane == k, r_k, rank)
    r_ref[...] = rank.astype(jnp.int32)
    carry[...] = carry[...] + jnp.sum(both, axis=0, keepdims=True)
    c_ref[...] = carry[...]


def _rank(eid):
    n_tok = eid.shape[0]
    tm = 512
    row = pl.BlockSpec((tm, HEAD_DIM), lambda i: (i, 0))
    return pl.pallas_call(
        _rank_kernel,
        grid=(n_tok // tm,),
        in_specs=[row],
        out_specs=[row, pl.BlockSpec((SUBLANES, HEAD_DIM), lambda i: (0, 0))],
        out_shape=[jax.ShapeDtypeStruct((n_tok, HEAD_DIM), jnp.int32), jax.ShapeDtypeStruct((SUBLANES, HEAD_DIM), F32)],
        scratch_shapes=[pltpu.VMEM((SUBLANES, HEAD_DIM), F32)],
        compiler_params=_params("arbitrary"),
        name="moe_rank",
    )(eid)


def _row_copies(pos_ref, slot0, tm, make_copy):
    def copy(r, k):
        return make_copy(r, k, pos_ref[slot0 + r * MOE_TOPK + k])

    def issue(r, c):
        for k in range(MOE_TOPK):
            copy(r, k).start()
        return c

    def drain(r, c):
        for k in range(MOE_TOPK):
            copy(r, k).wait()
        return c

    lax.fori_loop(0, tm, issue, 0, unroll=4)
    lax.fori_loop(0, tm, drain, 0, unroll=4)


def _dispatch_kernel(tok0, pos_ref, x_ref, rows_in, rows_out, sem):
    del rows_in
    tm = x_ref.shape[0]
    slot0 = (tok0 + pl.program_id(0) * tm) * MOE_TOPK
    _row_copies(pos_ref, slot0, tm, lambda r, k, dst: pltpu.make_async_copy(
        x_ref.at[pl.ds(r, 1), :], rows_out.at[pl.ds(dst, 1), :], sem))


def _dispatch(pos, x, rows, tok0):
    n_tok, dm = x.shape
    tm = min(n_tok, 512)
    grid_spec = pltpu.PrefetchScalarGridSpec(
        num_scalar_prefetch=1,
        grid=(n_tok // tm,),
        in_specs=[pl.BlockSpec((tm, dm), lambda i, pos: (i, 0)), pl.BlockSpec(memory_space=pl.ANY)],
        out_specs=pl.BlockSpec(memory_space=pl.ANY),
        scratch_shapes=[pltpu.SemaphoreType.DMA(())],
    )
    return pl.pallas_call(
        functools.partial(_dispatch_kernel, tok0),
        grid_spec=grid_spec,
        out_shape=jax.ShapeDtypeStruct(rows.shape, F32),
        input_output_aliases={2: 0},
        compiler_params=_params("arbitrary"),
        name="moe_dispatch",
    )(pos, x, rows)


def _ffn_kernel(be_ref, nv_ref, x_ref, wg_ref, wu_ref, wd_ref, o_ref):
    del be_ref

    @pl.when(pl.program_id(0) < nv_ref[0])
    def _():
        xb = x_ref[...].astype(BF16)
        g = _dot(xb, wg_ref[0].astype(BF16))
        u = _dot(xb, wu_ref[0].astype(BF16))
        hid = g * (1.0 / (1.0 + jnp.exp(-g))) * u
        o_ref[...] = _dot(hid.astype(BF16), wd_ref[0].astype(BF16))

    @pl.when(pl.program_id(0) >= nv_ref[0])
    def _():
        o_ref[...] = jnp.zeros_like(o_ref)


def _expert_ffn(rows, blk_exp, n_valid, wg, wu, wd):
    n_rows, dm = rows.shape
    hid = wg.shape[2]
    grid_spec = pltpu.PrefetchScalarGridSpec(
        num_scalar_prefetch=2,
        grid=(n_rows // MOE_ROWS,),
        in_specs=[pl.BlockSpec((MOE_ROWS, dm), lambda i, be, nv: (i, 0)),
                  pl.BlockSpec((1, dm, hid), lambda i, be, nv: (be[i], 0, 0)),
                  pl.BlockSpec((1, dm, hid), lambda i, be, nv: (be[i], 0, 0)),
                  pl.BlockSpec((1, hid, dm), lambda i, be, nv: (be[i], 0, 0))],
        out_specs=pl.BlockSpec((MOE_ROWS, dm), lambda i, be, nv: (i, 0)),
    )
    return pl.pallas_call(
        _ffn_kernel,
        grid_spec=grid_spec,
        out_shape=jax.ShapeDtypeStruct((n_rows, dm), F32),
        compiler_params=_params("arbitrary"),
        name="expert_ffn",
    )(blk_exp, n_valid, rows, wg, wu, wd)


def _combine_ln_kernel(tok0, pos_ref, x_ref, gate_ref, rows_ref, g_ref, b_ref, o_ref, buf, sem):
    tm = x_ref.shape[0]
    slot0 = (tok0 + pl.program_id(0) * tm) * MOE_TOPK
    _row_copies(pos_ref, slot0, tm, lambda r, k, src: pltpu.make_async_copy(
        rows_ref.at[pl.ds(src, 1), :], buf.at[k, pl.ds(r, 1), :], sem))
    gates = gate_ref[...]
    y = ALPHA * x_ref[...]
    for k in range(MOE_TOPK):
        y = y + gates[:, k:k + 1] * buf[k]
    o_ref[...] = _layer_norm(y, g_ref, b_ref)


def _combine_ln(pos, x, gates, rows, g, b, tok0):
    n_tok, dm = x.shape
    tm = 256
    row = pl.BlockSpec((tm, dm), lambda i, pos: (i, 0))
    vec = pl.BlockSpec((1, dm), lambda i, pos: (0, 0))
    grid_spec = pltpu.PrefetchScalarGridSpec(
        num_scalar_prefetch=1,
        grid=(n_tok // tm,),
        in_specs=[row, pl.BlockSpec((tm, HEAD_DIM), lambda i, pos: (i, 0)), pl.BlockSpec(memory_space=pl.ANY), vec, vec],
        out_specs=row,
        scratch_shapes=[pltpu.VMEM((MOE_TOPK, tm, dm), F32), pltpu.SemaphoreType.DMA(())],
    )
    return pl.pallas_call(
        functools.partial(_combine_ln_kernel, tok0),
        grid_spec=grid_spec,
        out_shape=jax.ShapeDtypeStruct((n_tok, dm), F32),
        compiler_params=_params("arbitrary"),
        name="moe_combine_ln",
    )(pos, x, gates, rows, g.reshape(1, dm), b.reshape(1, dm))


def _moe_ln(xp, xs, w_rg, b_rg, w_re, b_re, wg, wu, wd, ln_g, ln_b):
    dm = xp.shape[1]
    n_grp, _, epg = w_re.shape
    n_exp = wg.shape[0]
    n_col = n_grp + n_grp * epg
    w = jnp.concatenate([w_rg, w_re.transpose(1, 0, 2).reshape(dm, n_grp * epg)], axis=1)
    w = jnp.pad(w, ((0, 0), (0, HEAD_DIM - n_col)))
    b = jnp.pad(jnp.concatenate([b_rg, b_re.reshape(-1)]), (0, HEAD_DIM - n_col)).reshape(1, HEAD_DIM)
    eid_p, gate_p = _router(xp, w, b, n_grp, epg)
    eid_s, gate_s = _router(xs, w, b, n_grp, epg)
    eid = jnp.concatenate([eid_p, eid_s], axis=0)
    rank, counts = _rank(eid)
    counts = counts[0, :n_exp].astype(jnp.int32)
    pcounts = (counts + MOE_ROWS - 1) // MOE_ROWS * MOE_ROWS
    pends = jnp.cumsum(pcounts)
    pstarts = pends - pcounts
    choice = eid[:, :MOE_TOPK]
    start_of = jnp.sum(jnp.where(choice[..., None] == jnp.arange(n_exp, dtype=jnp.int32), pstarts, 0), axis=-1)
    pos = (rank[:, :MOE_TOPK] + start_of).reshape(-1)
    n_slot = pos.shape[0]
    n_blk = -(-(n_slot + n_exp * (MOE_ROWS - 1)) // MOE_ROWS)
    blk_start = jnp.arange(n_blk, dtype=jnp.int32) * MOE_ROWS
    blk_exp = jnp.minimum(jnp.sum(blk_start[:, None] >= pends[None, :], axis=1), n_exp - 1).astype(jnp.int32)
    n_valid = (pends[-1] // MOE_ROWS).astype(jnp.int32).reshape(1)
    rows = jnp.zeros((n_blk * MOE_ROWS, dm), F32)
    rows = _dispatch(pos, xp, rows, 0)
    rows = _dispatch(pos, xs, rows, xp.shape[0])
    rows = _expert_ffn(rows, blk_exp, n_valid, wg, wu, wd)
    return (_combine_ln(pos, xp, gate_p, rows, ln_g, ln_b, 0),
            _combine_ln(pos, xs, gate_s, rows, ln_g, ln_b, xp.shape[0]))


def kernel(x_prompt, x_sample, cache_kv_ab, state_kv_c_w128, state_kv_c_w512, state_kv_c_w2048, page_table, rel_bias,
           w_in_ab, w_out_ab, w_in_c, w_out_c, ln_mix_g, ln_mix_b, w_route_grp, b_route_grp, w_route_exp, b_route_exp,
           w_gate, w_up, w_down, ln_ffn_g, ln_ffn_b):
    bsz, seq, dm = x_prompt.shape
    nseq, t_len, _ = x_sample.shape
    n_pages = page_table.shape[1]
    past_len = n_pages * PAGE_SIZE
    c_states = (state_kv_c_w128, state_kv_c_w512, state_kv_c_w2048)
    xp = x_prompt.reshape(bsz * seq, dm)
    xs = x_sample.reshape(nseq * t_len, dm)
    d_ab = AB_HEADS * HEAD_DIM
    d_c = C_HEADS * HEAD_DIM
    d_cg = C_GROUP_HEADS * HEAD_DIM
    outs = {}

    for layer in range(DEPTH):
        li = layer // 2
        if layer % 2 == 0:
            w_in = w_in_ab[li].astype(BF16)
            w_out = w_out_ab[li].astype(BF16)
            qp = _matmul(xp, w_in, 0, 1, d_ab)
            kvp = _matmul(xp, w_in, d_ab // 512, 1, 2 * d_ab)
            qs = _matmul(xs, w_in, 0, 1, d_ab)
            kvs = _matmul(xs, w_in, d_ab // 512, 1, 2 * d_ab)
            att_p = _ab_prompt(qp, kvp, _moba_prompt_bias(rel_bias), bsz, seq)
            cache = cache_kv_ab[li].reshape(cache_kv_ab.shape[1], PAGE_SIZE, 4, SUBLANES, HEAD_DIM)
            tab, tabn = _moba_sample_tables(rel_bias, n_pages, t_len, past_len)
            att_s = _ab_sample(qs, kvs, cache, page_table, tab, tabn, t_len)
            mix_p = _matmul(att_p, w_out, 0, 1, dm)
            mix_s = _matmul(att_s, w_out, 0, 1, dm)
            outs["kv_ab_p"] = kvp.reshape(1, bsz, seq, 2, AB_HEADS, HEAD_DIM)
            outs["kv_ab_s"] = kvs.reshape(1, nseq, t_len, 2, AB_HEADS, HEAD_DIM)
        else:
            w_in = w_in_c[li].astype(BF16)
            w_out = w_out_c[li].astype(BF16)
            qp = _matmul(xp, w_in, 0, 1, d_c)
            qs = _matmul(xs, w_in, 0, 1, d_c)
            kvp, kvs = [], []
            for g in range(len(C_GROUPS)):
                kvp.append(_matmul(xp, w_in, d_c // 512 + g, d_c // 512, 2 * d_cg))
                kvs.append(_matmul(xs, w_in, d_c // 512 + g, d_c // 512, 2 * d_cg))
            att_p = _c_prompt(qp, kvp, _c_prompt_masks(rel_bias, seq), bsz, seq)
            q_pad = jnp.pad(qs.reshape(nseq * t_len, len(C_GROUPS), C_GROUP_HEADS, HEAD_DIM),
                            ((0, 0), (0, 0), (0, 8 - C_GROUP_HEADS), (0, 0)))
            kvn3 = [kv.reshape(nseq * t_len, 8, HEAD_DIM) for kv in kvs]
            st8 = [st[li][None].reshape(1, nseq, st.shape[2], 8, HEAD_DIM) for st in c_states]
            att_s, *next_states = _c_sample(q_pad, kvn3, st8, _c_sample_masks(rel_bias), t_len)
            att_s = att_s[:, :, :C_GROUP_HEADS].reshape(nseq * t_len, d_cg)
            mix_p = _matmul(att_p, w_out, 0, 1, dm)
            mix_s = _matmul(att_s, w_out, 0, 1, dm)
            for g, (win, _) in enumerate(C_GROUPS):
                keep = min(win, seq)
                kv5 = kvp[g].reshape(bsz, seq, 2, C_GROUP_HEADS, HEAD_DIM)
                outs[f"c_p{g}"] = kv5[:, seq - keep:][None]
                outs[f"c_s{g}"] = next_states[g].reshape((1, nseq, win, 2, C_GROUP_HEADS, HEAD_DIM))
        xp = _res_ln(xp, mix_p, ln_mix_g[layer], ln_mix_b[layer])
        xs = _res_ln(xs, mix_s, ln_mix_g[layer], ln_mix_b[layer])
        xp, xs = _moe_ln(xp, xs, w_route_grp[layer], b_route_grp[layer], w_route_exp[layer], b_route_exp[layer],
                         w_gate[layer], w_up[layer], w_down[layer], ln_ffn_g[layer], ln_ffn_b[layer])

    return (xp.reshape(bsz, seq, dm), xs.reshape(nseq, t_len, dm), outs["kv_ab_p"], outs["kv_ab_s"],
            outs["c_p0"], outs["c_s0"], outs["c_p1"], outs["c_s1"], outs["c_p2"], outs["c_s2"])
```

```python
import functools
import math

import jax
import jax.numpy as jnp
import numpy as np
from jax import lax
from jax.experimental import pallas as pl
from jax.experimental.pallas import tpu as pltpu

F32 = jnp.float32
BF16 = jnp.bfloat16
NEG_INF = float("-inf")
HIGHEST = lax.Precision.HIGHEST

HEAD_DIM = 128
MOBA_HEADS = 12
SB_HEADS = 4
AB_HEADS = MOBA_HEADS + SB_HEADS
MOBA_BLOCK = 256
MOBA_TOPK = 3
PAGE_SIZE = 128
C_GROUPS = ((128, 1), (512, 4), (2048, 16))
C_GROUP_HEADS = 4
C_HEADS = C_GROUP_HEADS * len(C_GROUPS)
N_BUCKETS = 32
REL_MAX_DIST = 128
MOE_TOPK = 2
DEPTH = 2
ALPHA = (2 * DEPTH) ** 0.25
LN_EPS = 1e-5
SCALE = HEAD_DIM ** -0.5

SUBLANES = 8
QBLK = 256
MOE_ROWS = 256
C_SLABS = 144
PAGES_PER_STEP = 4
N_NEAR = 1
MASK_PENALTY = 2.0 ** 100
VMEM_LIMIT = 56 * 1024 * 1024


def _dot(a, b, **kw):
    return jnp.dot(a, b, preferred_element_type=F32, **kw)


def _dot_nt(a, b, **kw):
    return lax.dot_general(a, b, (((1,), (1,)), ((), ())), preferred_element_type=F32, **kw)


def _split_dot(x, u):
    hi = x.astype(BF16)
    lo = (x - hi.astype(F32)).astype(BF16)
    return _dot(hi, u) + _dot(lo, u)


def _log_sigmoid(z):
    return jnp.minimum(z, 0.0) - jnp.log(1.0 + jnp.exp(-jnp.abs(z)))


def _params(*sem):
    return pltpu.CompilerParams(dimension_semantics=sem, vmem_limit_bytes=VMEM_LIMIT)


def _bucket_np(dist):
    dist = np.asarray(dist, np.int32)
    exact = N_BUCKETS // 2
    far = np.maximum(dist, exact).astype(np.float32)
    frac = np.log(far / np.float32(exact)) / np.float32(math.log(REL_MAX_DIST / exact))
    big = exact + (frac * np.float32(N_BUCKETS - exact)).astype(np.int32)
    return np.where(dist < exact, dist, np.minimum(big, N_BUCKETS - 1)).astype(np.int32)


def _bias_lookup(bias_cols, idx):
    flat = jnp.asarray(idx.reshape(-1))
    onehot = (jnp.arange(N_BUCKETS, dtype=jnp.int32)[:, None] == flat[None, :]).astype(F32)
    return jnp.dot(bias_cols.T, onehot, precision=HIGHEST).reshape((bias_cols.shape[1],) + idx.shape)


def _topk_select(gate, valid, n_cand):
    lane = lax.broadcasted_iota(jnp.int32, gate.shape, 1)
    sels = []
    for n in range(n_cand):
        g_n = gate[:, n:n + 1]
        beats = jnp.where(valid, jnp.where((gate > g_n) | ((gate == g_n) & (lane < n)), 1.0, 0.0), 0.0)
        sels.append(jnp.sum(beats, axis=1, keepdims=True) < MOBA_TOPK)
    return sels


def _mm_kernel(x_ref, w_ref, o_ref):
    o_ref[...] = _dot(x_ref[...].astype(BF16), w_ref[...].astype(BF16))


def _matmul(x, w, col0, cstep, ncols, tn=512):
    m, k = x.shape
    tm = min(m, 1024)
    return pl.pallas_call(
        _mm_kernel,
        grid=(m // tm, ncols // tn),
        in_specs=[pl.BlockSpec((tm, k), lambda i, j: (i, 0)),
                  pl.BlockSpec((k, tn), lambda i, j: (0, col0 + j * cstep))],
        out_specs=pl.BlockSpec((tm, tn), lambda i, j: (i, j)),
        out_shape=jax.ShapeDtypeStruct((m, ncols), F32),
        compiler_params=_params("parallel", "parallel"),
        name="matmul",
    )(x, w)


def _layer_norm(y, g_ref, b_ref):
    mu = jnp.mean(y, axis=-1, keepdims=True)
    yc = y - mu
    var = jnp.mean(yc * yc, axis=-1, keepdims=True)
    return yc * lax.rsqrt(var + LN_EPS) * g_ref[...] + b_ref[...]


def _res_ln_kernel(x_ref, a_ref, g_ref, b_ref, o_ref):
    o_ref[...] = _layer_norm(ALPHA * x_ref[...] + a_ref[...], g_ref, b_ref)


def _res_ln(x, a, g, b):
    m, d = x.shape
    tm = min(m, 512)
    row = pl.BlockSpec((tm, d), lambda i: (i, 0))
    vec = pl.BlockSpec((1, d), lambda i: (0, 0))
    return pl.pallas_call(
        _res_ln_kernel,
        grid=(m // tm,),
        in_specs=[row, row, vec, vec],
        out_specs=row,
        out_shape=jax.ShapeDtypeStruct((m, d), F32),
        compiler_params=_params("parallel"),
        name="res_ln",
    )(x, a, g.reshape(1, d), b.reshape(1, d))


def _ab_prompt_kernel(q_ref, k_ref, v_ref, bias_ref, o_ref, means_ref):
    h = pl.program_id(1)
    i = pl.program_id(2)
    n_blk = k_ref.shape[0] // QBLK
    row = lax.broadcasted_iota(jnp.int32, (QBLK, QBLK), 0)
    col = lax.broadcasted_iota(jnp.int32, (QBLK, QBLK), 1)
    n_steps = i // 2 + 1

    def kv_pair(jp):
        start = pl.multiple_of(jp * 2 * QBLK, 2 * QBLK)
        return k_ref[pl.ds(start, 2 * QBLK), :].astype(BF16), v_ref[pl.ds(start, 2 * QBLK), :].astype(BF16)

    def halves(x):
        return x[:, 0:QBLK], x[:, QBLK:2 * QBLK]

    @pl.when(h < MOBA_HEADS)
    def _moba():
        @pl.when(i == 0)
        def _():
            means_ref[...] = jnp.sum(k_ref[...].reshape(n_blk, QBLK, HEAD_DIM), axis=1) * (1.0 / QBLK)

        q = q_ref[...]
        gate = _dot_nt(means_ref[...], q, precision=HIGHEST)
        blk = lax.broadcasted_iota(jnp.int32, gate.shape, 0)
        sel_t = jnp.zeros(gate.shape, F32)
        for n in range(n_blk):
            g_n = gate[n:n + 1, :]
            beats = jnp.where(blk < i, jnp.where((gate > g_n) | ((gate == g_n) & (blk < n)), 1.0, 0.0), 0.0)
            top = jnp.where(jnp.sum(beats, axis=0, keepdims=True) < MOBA_TOPK, 1.0, 0.0) * jnp.where(n < i, 1.0, 0.0)
            sel_t = jnp.where(blk == n, top + jnp.where(n == i, 1.0, 0.0), sel_t)
        sel_t = jnp.concatenate([sel_t, jnp.zeros((HEAD_DIM - n_blk, QBLK), F32)], axis=0).astype(BF16)
        sel = _dot_nt(jnp.where(row == col, 1.0, 0.0).astype(BF16), sel_t)
        drop = (sel - 1.0).astype(BF16)

        qs = (q * SCALE).astype(BF16)
        pen_blk = lax.broadcasted_iota(jnp.int32, (SUBLANES, 2 * QBLK), 0)
        pen_half = lax.broadcasted_iota(jnp.int32, (SUBLANES, 2 * QBLK), 1) // QBLK
        pen_pad = jnp.zeros((HEAD_DIM - SUBLANES, 2 * QBLK), F32)

        def body(jj, carry):
            m, l, acc = carry
            jp = i // 2 - jj
            k_t, v_t = kv_pair(jp)
            pen = jnp.where(pen_blk - 2 * jp == pen_half, MASK_PENALTY, 0.0)
            pen = jnp.concatenate([pen, pen_pad], axis=0).astype(BF16)
            s = _dot_nt(qs, k_t) + _dot(drop, pen)
            s = s + jnp.concatenate([bias_ref[0, jnp.clip(i - 2 * jp - half, 0, 2)] for half in range(2)], axis=1)
            m_new = jnp.maximum(m, jnp.max(s, axis=1, keepdims=True))
            a = jnp.exp(m - m_new)
            p = jnp.exp(s - m_new)
            return m_new, a * l + jnp.sum(p, axis=1, keepdims=True), a * acc + _dot(p.astype(BF16), v_t)

        init = (jnp.full((QBLK, 1), NEG_INF, F32), jnp.zeros((QBLK, 1), F32), jnp.zeros((QBLK, HEAD_DIM), F32))
        m, l, acc = lax.fori_loop(0, n_steps, body, init)
        o_ref[...] = acc / l

    @pl.when(h >= MOBA_HEADS)
    def _stick_breaking():
        qs = (q_ref[...] * SCALE).astype(BF16)
        upper = jnp.where(row > col, 1.0, 0.0).astype(BF16)

        def body(jj, carry, own_pair):
            later, acc = carry
            jp = i // 2 - jj
            k_t, v_t = kv_pair(jp)
            z = _dot_nt(qs, k_t)
            ls = _log_sigmoid(z)
            lks = list(halves(ls - z))
            if own_pair:
                keeps = []
                for half in range(2):
                    off = i - (2 * jp + half)
                    keeps.append(jnp.where(off == 0, jnp.where(col < row, 1.0, 0.0), jnp.where(off > 0, 1.0, 0.0)) > 0.5)
                    lks[half] = jnp.where(keeps[half], lks[half], 0.0)
            inner = _split_dot(jnp.concatenate(lks, axis=0), upper)
            total = [jnp.sum(x, axis=1, keepdims=True) for x in lks]
            after = (inner[0:QBLK] + (later + total[1]), inner[QBLK:2 * QBLK] + later)
            ws = [jnp.exp(ls_h + after[half]) for half, ls_h in enumerate(halves(ls))]
            if own_pair:
                ws = [jnp.where(keeps[half], ws[half], 0.0) for half in range(2)]
            acc = acc + _dot(jnp.concatenate(ws, axis=1).astype(BF16), v_t)
            return later + total[0] + total[1], acc

        init = (jnp.zeros((QBLK, 1), F32), jnp.zeros((QBLK, HEAD_DIM), F32))
        carry = body(0, init, True)
        o_ref[...] = lax.fori_loop(1, n_steps, functools.partial(body, own_pair=False), carry)[1]


def _ab_prompt(q, kv, bias_tiles, bsz, seq):
    nq = seq // QBLK
    assert seq % (2 * QBLK) == 0 and nq <= SUBLANES
    return pl.pallas_call(
        _ab_prompt_kernel,
        grid=(bsz, AB_HEADS, nq),
        in_specs=[pl.BlockSpec((QBLK, HEAD_DIM), lambda b, h, i: (b * nq + i, h)),
                  pl.BlockSpec((seq, HEAD_DIM), lambda b, h, i: (b, h)),
                  pl.BlockSpec((seq, HEAD_DIM), lambda b, h, i: (b, AB_HEADS + h)),
                  pl.BlockSpec((1, 3, QBLK, QBLK), lambda b, h, i: (jnp.minimum(h, MOBA_HEADS - 1), 0, 0, 0))],
        out_specs=pl.BlockSpec((QBLK, HEAD_DIM), lambda b, h, i: (b * nq + i, h)),
        out_shape=jax.ShapeDtypeStruct(q.shape, F32),
        scratch_shapes=[pltpu.VMEM((seq // QBLK, HEAD_DIM), F32)],
        compiler_params=_params("parallel", "parallel", "arbitrary"),
        name="ab_prompt",
    )(q, kv, kv, bias_tiles)


def _moba_prompt_bias(rel_bias):
    t = np.arange(QBLK)
    dist = np.arange(3)[:, None, None] * QBLK + t[None, :, None] - t[None, None, :]
    return jnp.where((dist >= 0)[None], _bias_lookup(rel_bias, _bucket_np(np.maximum(dist, 0))), NEG_INF)


def _ab_sample_kernel(pt_ref, q_ref, kvn_ref, *refs):
    del pt_ref
    n_pg = PAGES_PER_STEP
    page_refs = refs[:n_pg]
    tab_ref, tabn_ref, o_ref, qs_sc, qg_sc, part_sc, bsum, kscr, vscr, sb_carry, sb_acc = refs[n_pg:]
    s = pl.program_id(1)
    t_len = q_ref.shape[0]
    pages_per_block = MOBA_BLOCK // PAGE_SIZE
    n_blocks = part_sc.shape[0] - 1
    n_pages = n_blocks * pages_per_block
    n_steps = 1 + n_pages // PAGES_PER_STEP
    rows = SUBLANES * t_len
    page_rows = PAGE_SIZE * SUBLANES
    moba_rows = (rows, (MOBA_HEADS - SUBLANES) * t_len)
    eye_r = lax.broadcasted_iota(jnp.int32, (PAGE_SIZE, PAGE_SIZE), 0)
    eye_c = lax.broadcasted_iota(jnp.int32, (PAGE_SIZE, PAGE_SIZE), 1)
    upper = jnp.where(eye_r > eye_c, 1.0, 0.0).astype(BF16)

    def head_cols(ref, h):
        return ref[:, h * HEAD_DIM:(h + 1) * HEAD_DIM]

    def moba_partial(slot, hg, pieces):
        nr = moba_rows[hg]
        qs = qs_sc[hg, 0:nr]
        scores = [_dot_nt(qs, kx) + table for kx, _, table in pieces]
        m = functools.reduce(jnp.maximum, [jnp.max(sc, axis=1, keepdims=True) for sc in scores])
        es = [jnp.exp(sc - m) for sc in scores]
        part_sc[slot, hg, 0, 0:nr] = jnp.broadcast_to(m, (nr, HEAD_DIM))
        part_sc[slot, hg, 1, 0:nr] = jnp.broadcast_to(sum(jnp.sum(e, axis=1, keepdims=True) for e in es), (nr, HEAD_DIM))
        part_sc[slot, hg, 2, 0:nr] = sum(_dot(e, vx) for e, (_, vx, _) in zip(es, pieces))

    def sb_step(keys, values, causal):
        n_chunk = keys[0].shape[0] // PAGE_SIZE
        zs = []
        for j in range(SB_HEADS):
            z = _dot_nt(head_cols(q_ref, MOBA_HEADS + j) * SCALE, keys[j])
            zs += [z[:, c * PAGE_SIZE:(c + 1) * PAGE_SIZE] for c in range(n_chunk)]
        z = jnp.concatenate(zs, axis=0)
        ls = _log_sigmoid(z)
        lk = ls - z
        if causal:
            tq = lax.broadcasted_iota(jnp.int32, z.shape, 0) % t_len
            valid = lax.broadcasted_iota(jnp.int32, z.shape, 1) < tq
            lk = jnp.where(valid, lk, 0.0)
        inner = ls + _split_dot(lk, upper)
        total = jnp.sum(lk, axis=1, keepdims=True)
        for j in range(SB_HEADS):
            later = sb_carry[j]
            ws = [None] * n_chunk
            for c in reversed(range(n_chunk)):
                r0 = (j * n_chunk + c) * t_len
                w = jnp.exp(inner[r0:r0 + t_len] + later)
                ws[c] = jnp.where(valid[r0:r0 + t_len], w, 0.0) if causal else w
                later = later + total[r0:r0 + t_len]
            sb_carry[j] = later
            sb_acc[j] = sb_acc[j] + _dot(jnp.concatenate(ws, axis=1), values[j])

    @pl.when(s == 0)
    def _new_tokens():
        bsum[...] = jnp.zeros_like(bsum)
        sb_carry[...] = jnp.zeros_like(sb_carry)
        sb_acc[...] = jnp.zeros_like(sb_acc)
        pad = jnp.zeros((HEAD_DIM - rows, HEAD_DIM), F32)
        for hg in range(2):
            heads = range(hg * SUBLANES, (hg + 1) * SUBLANES)
            qst = jnp.concatenate([head_cols(q_ref, h) for h in heads], axis=0)
            qg_sc[hg] = qst
            qs_sc[hg] = qst * SCALE
            kx = jnp.concatenate([head_cols(kvn_ref, h) for h in heads] + [pad], axis=0)
            vx = jnp.concatenate([head_cols(kvn_ref, AB_HEADS + h) for h in heads] + [pad], axis=0)
            moba_partial(0, hg, [(kx, vx, tabn_ref[hg, 0:moba_rows[hg]])])
        pad = jnp.zeros((PAGE_SIZE - t_len, HEAD_DIM), F32)
        sb_step([jnp.concatenate([head_cols(kvn_ref, MOBA_HEADS + j), pad], axis=0) for j in range(SB_HEADS)],
                [jnp.concatenate([head_cols(kvn_ref, AB_HEADS + MOBA_HEADS + j), pad], axis=0) for j in range(SB_HEADS)],
                True)

    @pl.when(s > 0)
    def _pages():
        page0 = n_pages - 1 - (s - 1) * PAGES_PER_STEP
        for ub in range(0, PAGES_PER_STEP, pages_per_block):
            blk = (page0 - ub) // pages_per_block
            for hg in range(2):
                pieces = []
                for u in reversed(range(ub, ub + pages_per_block)):
                    near = n_pages - (page0 - u)
                    tix = jnp.where(near <= N_NEAR, near, 0)
                    k3 = page_refs[u][0, :, hg]
                    bsum[hg, blk] = bsum[hg, blk] + jnp.sum(k3, axis=0)
                    kx = k3.reshape(page_rows, HEAD_DIM)
                    vx = page_refs[u][0, :, 2 + hg].reshape(page_rows, HEAD_DIM)
                    pieces.append((kx, vx, tab_ref[tix, hg, 0:moba_rows[hg]]))
                    if hg == 1:
                        kscr[u * page_rows:(u + 1) * page_rows] = kx
                        vscr[u * page_rows:(u + 1) * page_rows] = vx
                moba_partial(blk + 1, hg, pieces)

        def sb_rows(scr, j):
            h8 = MOBA_HEADS - SUBLANES + j
            return jnp.concatenate([scr[pl.ds(u * page_rows + h8, PAGE_SIZE, stride=SUBLANES), :]
                                    for u in reversed(range(PAGES_PER_STEP))], axis=0)

        sb_step([sb_rows(kscr, j) for j in range(SB_HEADS)], [sb_rows(vscr, j) for j in range(SB_HEADS)], False)

    @pl.when(s == n_steps - 1)
    def _finish():
        fold = jnp.where(eye_r // SUBLANES == eye_c, 1.0, 0.0)
        for hg in range(2):
            nr = moba_rows[hg]
            lane = lax.broadcasted_iota(jnp.int32, (nr, HEAD_DIM), 1)
            rowi = lax.broadcasted_iota(jnp.int32, (nr, HEAD_DIM), 0)
            g_all = _dot_nt(qg_sc[hg, 0:nr], bsum[hg].reshape(HEAD_DIM, HEAD_DIM), precision=HIGHEST)
            g_own = jnp.where(lane % SUBLANES == rowi // t_len, g_all, 0.0)
            gate = _dot(g_own, fold, precision=HIGHEST) * (1.0 / MOBA_BLOCK)
            sels = _topk_select(gate, lane < n_blocks, n_blocks)

            def part(slot, k, hg=hg, nr=nr):
                return part_sc[slot, hg, k, 0:nr]

            m_all = part(0, 0)
            for n in range(n_blocks):
                m_all = jnp.maximum(m_all, jnp.where(sels[n], part(n + 1, 0), NEG_INF))
            wgt = jnp.exp(part(0, 0) - m_all)
            num = wgt * part(0, 2)
            den = wgt * part(0, 1)
            for n in range(n_blocks):
                wgt = jnp.where(sels[n], jnp.exp(part(n + 1, 0) - m_all), 0.0)
                num = num + wgt * part(n + 1, 2)
                den = den + wgt * part(n + 1, 1)
            out = num / den
            for h8 in range(nr // t_len):
                h = hg * SUBLANES + h8
                o_ref[:, h * HEAD_DIM:(h + 1) * HEAD_DIM] = out[h8 * t_len:(h8 + 1) * t_len]
        for j in range(SB_HEADS):
            o_ref[:, (MOBA_HEADS + j) * HEAD_DIM:(MOBA_HEADS + j + 1) * HEAD_DIM] = sb_acc[j]


def _ab_sample(q, kvn, cache, page_table, tab, tabn, t_len):
    assert AB_HEADS == 2 * SUBLANES and SUBLANES <= MOBA_HEADS and MOBA_BLOCK % PAGE_SIZE == 0
    nseq, n_pages = page_table.shape
    assert n_pages % PAGES_PER_STEP == 0 and PAGES_PER_STEP * PAGE_SIZE % MOBA_BLOCK == 0
    n_blocks = n_pages * PAGE_SIZE // MOBA_BLOCK
    n_steps = 1 + n_pages // PAGES_PER_STEP
    d_all = AB_HEADS * HEAD_DIM
    rows = SUBLANES * t_len

    def page_map(b, s, pt, *, u):
        page = n_pages - 1 - ((jnp.maximum(s, 1) - 1) * PAGES_PER_STEP + u)
        return (pt[b * n_pages + page], 0, 0, 0, 0)

    page_specs = [pl.BlockSpec((1, PAGE_SIZE, 4, SUBLANES, HEAD_DIM), functools.partial(page_map, u=u))
                  for u in range(PAGES_PER_STEP)]
    grid_spec = pltpu.PrefetchScalarGridSpec(
        num_scalar_prefetch=1,
        grid=(nseq, n_steps),
        in_specs=[pl.BlockSpec((t_len, d_all), lambda b, s, pt: (b, 0)),
                  pl.BlockSpec((t_len, 2 * d_all), lambda b, s, pt: (b, 0))] + page_specs +
                 [pl.BlockSpec(tab.shape, lambda b, s, pt: (0, 0, 0, 0)),
                  pl.BlockSpec(tabn.shape, lambda b, s, pt: (0, 0, 0))],
        out_specs=pl.BlockSpec((t_len, d_all), lambda b, s, pt: (b, 0)),
        scratch_shapes=[pltpu.VMEM((2, rows, HEAD_DIM), F32),
                        pltpu.VMEM((2, rows, HEAD_DIM), F32),
                        pltpu.VMEM((n_blocks + 1, 2, 3, rows, HEAD_DIM), F32),
                        pltpu.VMEM((2, HEAD_DIM // SUBLANES, SUBLANES, HEAD_DIM), F32),
                        pltpu.VMEM((PAGES_PER_STEP * PAGE_SIZE * SUBLANES, HEAD_DIM), F32),
                        pltpu.VMEM((PAGES_PER_STEP * PAGE_SIZE * SUBLANES, HEAD_DIM), F32),
                        pltpu.VMEM((SB_HEADS, t_len, HEAD_DIM), F32),
                        pltpu.VMEM((SB_HEADS, t_len, HEAD_DIM), F32)],
    )
    return pl.pallas_call(
        _ab_sample_kernel,
        grid_spec=grid_spec,
        out_shape=jax.ShapeDtypeStruct((nseq * t_len, d_all), F32),
        compiler_params=_params("parallel", "arbitrary"),
        name="ab_sample",
    )(page_table.reshape(-1), q, kvn, *([cache] * PAGES_PER_STEP), tab, tabn)


def _moba_sample_tables(rel_bias, n_pages, t_len, past_len):
    assert N_NEAR * PAGE_SIZE + 1 >= REL_MAX_DIST and past_len == n_pages * PAGE_SIZE
    t = np.arange(t_len)
    pos = np.arange(PAGE_SIZE)
    dist = [np.full((t_len, PAGE_SIZE), REL_MAX_DIST)]
    for near in range(1, N_NEAR + 1):
        dist.append(past_len + t[:, None] - ((n_pages - near) * PAGE_SIZE + pos[None, :]))
    idx = _bucket_np(np.maximum(np.stack(dist), 0))
    n_tab = idx.shape[0]
    same = np.eye(SUBLANES, dtype=bool)
    pad_heads = ((0, AB_HEADS - MOBA_HEADS),)
    bias = jnp.pad(_bias_lookup(rel_bias, idx), pad_heads + ((0, 0),) * 3)
    bias = bias.reshape(2, SUBLANES, n_tab, t_len, PAGE_SIZE).transpose(2, 0, 1, 3, 4)
    tab = jnp.where(same[None, None, :, None, None, :], bias[..., None], NEG_INF)
    tab = tab.reshape(n_tab, 2, SUBLANES * t_len, PAGE_SIZE * SUBLANES)
    idx_n = _bucket_np(np.maximum(t[:, None] - t[None, :], 0))
    bias_n = jnp.pad(_bias_lookup(rel_bias, idx_n), pad_heads + ((0, 0),) * 2).reshape(2, SUBLANES, t_len, t_len)
    ok = same[:, None, :, None] & (t[None, :, None, None] >= t[None, None, None, :])
    tabn = jnp.where(ok[None], bias_n[:, :, :, None, :], NEG_INF).reshape(2, SUBLANES * t_len, SUBLANES * t_len)
    tabn = jnp.pad(tabn, ((0, 0), (0, 0), (0, HEAD_DIM - SUBLANES * t_len)), constant_values=NEG_INF)
    return tab, tabn


def _c_tile_counts(seq):
    return tuple(min(win // QBLK + 1, seq // QBLK) if win >= QBLK else 2 for win, _ in C_GROUPS)


def _c_prompt_kernel(counts, *refs):
    q_refs, kv_refs, (mask_ref, o_ref) = refs[0:3], refs[3:9], refs[9:]
    i = pl.program_id(2)
    n_q = kv_refs[0].shape[0] // QBLK
    n_steps = i // 2 + 1
    qs = [(q_ref[...] * SCALE).astype(BF16) for q_ref in q_refs]
    tile0 = [sum(counts[:g]) + g for g in range(len(counts))]

    def step(jj, carry, groups):
        m, l, acc = carry
        jp = i // 2 - jj
        start = pl.multiple_of(jp * 2 * QBLK, 2 * QBLK)
        scores, values = [], []
        for g in groups:
            k_t = kv_refs[2 * g][pl.ds(start, 2 * QBLK), :].astype(BF16)
            values.append(kv_refs[2 * g + 1][pl.ds(start, 2 * QBLK), :].astype(BF16))
            tiles = []
            for half in range(2):
                off = i - (2 * jp + half)
                tiles.append(mask_ref[0, tile0[g] + jnp.where((off >= 0) & (off < counts[g]), off, counts[g])])
            scores.append(_dot_nt(qs[g], k_t) + jnp.concatenate(tiles, axis=1))
        m_new = functools.reduce(jnp.maximum, [m] + [jnp.max(sc, axis=1, keepdims=True) for sc in scores])
        a = jnp.exp(m - m_new)
        ps = [jnp.exp(sc - m_new) for sc in scores]
        l = a * l + sum(jnp.sum(p, axis=1, keepdims=True) for p in ps)
        acc = a * acc + sum(_dot(p.astype(BF16), v_t) for p, v_t in zip(ps, values))
        return m_new, l, acc

    every = tuple(range(len(counts)))
    wide = tuple(g for g in every if counts[g] == n_q)
    short_steps = max((counts[g] + 2) // 2 for g in every if g not in wide)
    state = (jnp.full((QBLK, 1), NEG_INF, F32), jnp.zeros((QBLK, 1), F32), jnp.zeros((QBLK, HEAD_DIM), F32))
    state = lax.fori_loop(0, jnp.minimum(n_steps, short_steps), functools.partial(step, groups=every), state)
    if wide:
        state = lax.fori_loop(short_steps, n_steps, functools.partial(step, groups=wide), state)
    m, l, acc = state
    o_ref[...] = acc / l


def _c_prompt(q, kvs, mask_tiles, bsz, seq):
    nq = seq // QBLK
    counts = _c_tile_counts(seq)
    q_specs = [pl.BlockSpec((QBLK, HEAD_DIM), lambda b, j, i, g=g: (b * nq + i, C_GROUP_HEADS * g + j))
               for g in range(len(C_GROUPS))]
    kv_specs = []
    for g in range(len(C_GROUPS)):
        kv_specs.append(pl.BlockSpec((seq, HEAD_DIM), lambda b, j, i: (b, j)))
        kv_specs.append(pl.BlockSpec((seq, HEAD_DIM), lambda b, j, i: (b, C_GROUP_HEADS + j)))
    kv_args = [kvs[g] for g in range(len(C_GROUPS)) for _ in range(2)]
    return pl.pallas_call(
        functools.partial(_c_prompt_kernel, counts),
        grid=(bsz, C_GROUP_HEADS, nq),
        in_specs=q_specs + kv_specs + [pl.BlockSpec((1,) + mask_tiles.shape[1:], lambda b, j, i: (j, 0, 0, 0))],
        out_specs=pl.BlockSpec((QBLK, HEAD_DIM), lambda b, j, i: (b * nq + i, j)),
        out_shape=jax.ShapeDtypeStruct((bsz * seq, C_GROUP_HEADS * HEAD_DIM), F32),
        compiler_params=_params("parallel", "parallel", "arbitrary"),
        name="c_prompt",
    )(q, q, q, *kv_args, mask_tiles)


def _c_prompt_masks(rel_bias, seq):
    t = np.arange(QBLK)
    tiles = []
    for g, ((win, dil), n_tiles) in enumerate(zip(C_GROUPS, _c_tile_counts(seq))):
        d = np.arange(n_tiles)[:, None, None] * QBLK + t[None, :, None] - t[None, None, :]
        ok = (d >= 0) & (d <= win) & (d % dil == 0)
        bias = _bias_lookup(rel_bias[:, C_GROUP_HEADS * g:C_GROUP_HEADS * (g + 1)], _bucket_np(np.maximum(d, 0)))
        tiles.append(jnp.where(ok[None], bias, NEG_INF))
        tiles.append(jnp.full((C_GROUP_HEADS, 1, QBLK, QBLK), NEG_INF, F32))
    return jnp.concatenate(tiles, axis=1)


def _c_sample_kernel(q_ref, *refs):
    n_g = len(C_GROUPS)
    kvn_refs, st_refs = refs[0:n_g], refs[n_g:2 * n_g]
    mask_ref, o_ref = refs[2 * n_g], refs[2 * n_g + 1]
    next_refs = refs[2 * n_g + 2:]
    t_len = q_ref.shape[0]
    for g in range(n_g):
        n_buf = st_refs[g].shape[2]
        next_refs[g][0, 0, 0:n_buf - t_len] = st_refs[g][0, 0, t_len:n_buf]
        next_refs[g][0, 0, n_buf - t_len:n_buf] = kvn_refs[g][...]
    for t in range(t_len):
        xs, scores = [], []
        for g, (win, dil) in enumerate(C_GROUPS):
            n_buf = st_refs[g].shape[2]
            m_max = win // dil
            first = n_buf + t - m_max * dil
            n_old = -(-(n_buf - first) // dil)
            old = st_refs[g][0, 0, pl.ds(first, n_old, stride=dil)]
            new = [kvn_refs[g][first + c * dil - n_buf][None] for c in range(n_old, m_max + 1)]
            pad = jnp.zeros((C_SLABS - m_max - 1, 8, HEAD_DIM), F32)
            x = jnp.concatenate([old] + new + [pad], axis=0).reshape(C_SLABS * 8, HEAD_DIM)
            xs.append(x)
            scores.append(_dot_nt(q_ref[t, g] * SCALE, x) + mask_ref[g])
        m = functools.reduce(jnp.maximum, [jnp.max(sc, axis=1, keepdims=True) for sc in scores])
        ps = [jnp.exp(sc - m) for sc in scores]
        l = sum(jnp.sum(p, axis=1, keepdims=True) for p in ps)
        acc = sum(_dot(pltpu.roll(p, C_GROUP_HEADS, axis=1), x) for p, x in zip(ps, xs))
        o_ref[0, t] = acc / l


def _c_sample(q, kvns, states, masks, t_len):
    nseq = states[0].shape[1]
    n_g = len(C_GROUPS)
    in_specs = [pl.BlockSpec((t_len, n_g, 8, HEAD_DIM), lambda b: (b, 0, 0, 0))]
    in_specs += [pl.BlockSpec((t_len, 8, HEAD_DIM), lambda b: (b, 0, 0)) for _ in range(n_g)]
    in_specs += [pl.BlockSpec((1, 1) + st.shape[2:], lambda b: (0, b, 0, 0, 0)) for st in states]
    in_specs += [pl.BlockSpec(masks.shape, lambda b: (0, 0, 0))]
    for st, (win, dil) in zip(states, C_GROUPS):
        assert st.shape[2] == win and win // dil < C_SLABS
    st_specs = [pl.BlockSpec((1, 1) + st.shape[2:], lambda b: (0, b, 0, 0, 0)) for st in states]
    return pl.pallas_call(
        _c_sample_kernel,
        grid=(nseq,),
        in_specs=in_specs,
        out_specs=[pl.BlockSpec((1, t_len, 8, HEAD_DIM), lambda b: (b, 0, 0, 0))] + st_specs,
        out_shape=[jax.ShapeDtypeStruct((nseq, t_len, 8, HEAD_DIM), F32)] +
                  [jax.ShapeDtypeStruct(st.shape, F32) for st in states],
        compiler_params=_params("parallel"),
        name="c_sample",
    )(q, *kvns, *states, masks)


def _c_sample_masks(rel_bias):
    c = np.arange(C_SLABS * 8)
    slab, is_v, head = c // 8, (c % 8) // C_GROUP_HEADS, c % C_GROUP_HEADS
    slot = np.arange(8)
    out = []
    for g, (win, dil) in enumerate(C_GROUPS):
        m_max = win // dil
        steps = m_max - slab
        ok = (steps >= 0)[None, :] & (is_v == 0)[None, :] & (head[None, :] == slot[:, None])
        bias = _bias_lookup(rel_bias[:, C_GROUP_HEADS * g:C_GROUP_HEADS * (g + 1)],
                            _bucket_np(np.maximum(steps, 0) * dil))
        bias = jnp.concatenate([bias, jnp.zeros_like(bias)], axis=0)
        out.append(jnp.where(slot[:, None] < C_GROUP_HEADS, jnp.where(ok, bias, NEG_INF), 0.0))
    return jnp.stack(out)


def _router_kernel(n_grp, epg, x_ref, w_ref, b_ref, e_ref, g_ref):
    logits = _dot(x_ref[...], w_ref[...], precision=HIGHEST) + b_ref[...]
    lane = lax.broadcasted_iota(jnp.int32, logits.shape, 1)
    big = jnp.int32(2 ** 30)

    def first_max(valid):
        v = jnp.max(jnp.where(valid, logits, NEG_INF), axis=1, keepdims=True)
        idx = jnp.min(jnp.where(valid & (logits == v), lane, big), axis=1, keepdims=True)
        return v, idx

    in_grp = lane < n_grp
    g_max, grp = first_max(in_grp)
    p_grp = 1.0 / jnp.sum(jnp.where(in_grp, jnp.exp(logits - g_max), 0.0), axis=1, keepdims=True)
    lo = n_grp + grp * epg
    in_exp = (lane >= lo) & (lane < lo + epg)
    v1, i1 = first_max(in_exp)
    v2, i2 = first_max(in_exp & (lane != i1))
    e2 = jnp.exp(v2 - v1)
    g1 = p_grp / (1.0 + e2)
    g2 = p_grp * e2 / (1.0 + e2)
    e_ref[...] = jnp.where(lane == 0, i1 - n_grp, jnp.where(lane == 1, i2 - n_grp, 0))
    g_ref[...] = jnp.where(lane == 0, g1, jnp.where(lane == 1, g2, 0.0))


def _router(x, w, b, n_grp, epg):
    n_tok, dm = x.shape
    tm = min(n_tok, 512)
    row = pl.BlockSpec((tm, HEAD_DIM), lambda i: (i, 0))
    return pl.pallas_call(
        functools.partial(_router_kernel, n_grp, epg),
        grid=(n_tok // tm,),
        in_specs=[pl.BlockSpec((tm, dm), lambda i: (i, 0)),
                  pl.BlockSpec((dm, HEAD_DIM), lambda i: (0, 0)),
                  pl.BlockSpec((1, HEAD_DIM), lambda i: (0, 0))],
        out_specs=[row, row],
        out_shape=[jax.ShapeDtypeStruct((n_tok, HEAD_DIM), jnp.int32), jax.ShapeDtypeStruct((n_tok, HEAD_DIM), F32)],
        compiler_params=_params("parallel"),
        name="router",
    )(x, w, b)


def _rank_kernel(e_ref, r_ref, c_ref, carry):
    @pl.when(pl.program_id(0) == 0)
    def _():
        carry[...] = jnp.zeros_like(carry)

    e = e_ref[...]
    tm = e.shape[0]
    lane = lax.broadcasted_iota(jnp.int32, e.shape, 1)
    hits = [lane == e[:, k:k + 1] for k in range(MOE_TOPK)]
    both = sum(jnp.where(hit, 1.0, 0.0) for hit in hits)
    earlier = jnp.where(lax.broadcasted_iota(jnp.int32, (tm, tm), 0) > lax.broadcasted_iota(jnp.int32, (tm, tm), 1),
                        1.0, 0.0).astype(BF16)
    prefix = _dot(earlier, both.astype(BF16)) + carry[0:1, :]
    rank = jnp.zeros(e.shape, F32)
    for k, hit in enumerate(hits):
        r_k = jnp.sum(jnp.where(hit, prefix, 0.0), axis=1, keepdims=True)
        if k:
            r_k = r_k + sum(jnp.where(e[:, k:k + 1] == e[:, kk:kk + 1], 1.0, 0.0) for kk in range(k))
        rank = jnp.where(lane == k, r_k, rank)
    r_ref[...] = rank.astype(jnp.int32)
    carry[...] = carry[...] + jnp.sum(both, axis=0, keepdims=True)
    c_ref[...] = carry[...]


def _rank(eid):
    n_tok = eid.shape[0]
    tm = 512
    row = pl.BlockSpec((tm, HEAD_DIM), lambda i: (i, 0))
    return pl.pallas_call(
        _rank_kernel,
        grid=(n_tok // tm,),
        in_specs=[row],
        out_specs=[row, pl.BlockSpec((SUBLANES, HEAD_DIM), lambda i: (0, 0))],
        out_shape=[jax.ShapeDtypeStruct((n_tok, HEAD_DIM), jnp.int32), jax.ShapeDtypeStruct((SUBLANES, HEAD_DIM), F32)],
        scratch_shapes=[pltpu.VMEM((SUBLANES, HEAD_DIM), F32)],
        compiler_params=_params("arbitrary"),
        name="moe_rank",
    )(eid)


def _row_copies(pos_ref, slot0, tm, make_copy):
    def copy(r, k):
        return make_copy(r, k, pos_ref[slot0 + r * MOE_TOPK + k])

    def issue(r, c):
        for k in range(MOE_TOPK):
            copy(r, k).start()
        return c

    def drain(r, c):
        for k in range(MOE_TOPK):
            copy(r, k).wait()
        return c

    lax.fori_loop(0, tm, issue, 0, unroll=4)
    lax.fori_loop(0, tm, drain, 0, unroll=4)


def _dispatch_kernel(tok0, pos_ref, x_ref, rows_in, rows_out, sem):
    del rows_in
    tm = x_ref.shape[0]
    slot0 = (tok0 + pl.program_id(0) * tm) * MOE_TOPK
    _row_copies(pos_ref, slot0, tm, lambda r, k, dst: pltpu.make_async_copy(
        x_ref.at[pl.ds(r, 1), :], rows_out.at[pl.ds(dst, 1), :], sem))


def _dispatch(pos, x, rows, tok0):
    n_tok, dm = x.shape
    tm = min(n_tok, 512)
    grid_spec = pltpu.PrefetchScalarGridSpec(
        num_scalar_prefetch=1,
        grid=(n_tok // tm,),
        in_specs=[pl.BlockSpec((tm, dm), lambda i, pos: (i, 0)), pl.BlockSpec(memory_space=pl.ANY)],
        out_specs=pl.BlockSpec(memory_space=pl.ANY),
        scratch_shapes=[pltpu.SemaphoreType.DMA(())],
    )
    return pl.pallas_call(
        functools.partial(_dispatch_kernel, tok0),
        grid_spec=grid_spec,
        out_shape=jax.ShapeDtypeStruct(rows.shape, F32),
        input_output_aliases={2: 0},
        compiler_params=_params("arbitrary"),
        name="moe_dispatch",
    )(pos, x, rows)


def _ffn_kernel(be_ref, nv_ref, x_ref, wg_ref, wu_ref, wd_ref, o_ref):
    del be_ref

    @pl.when(pl.program_id(0) < nv_ref[0])
    def _():
        xb = x_ref[...].astype(BF16)
        g = _dot(xb, wg_ref[0, 0].astype(BF16))
        u = _dot(xb, wu_ref[0, 0].astype(BF16))
        hid = g * (1.0 / (1.0 + jnp.exp(-g))) * u
        o_ref[...] = _dot(hid.astype(BF16), wd_ref[0, 0].astype(BF16))

    @pl.when(pl.program_id(0) >= nv_ref[0])
    def _():
        o_ref[...] = jnp.zeros_like(o_ref)


def _expert_ffn(rows, blk_exp, n_valid, wg, wu, wd, layer):
    n_rows, dm = rows.shape
    hid = wg.shape[3]
    grid_spec = pltpu.PrefetchScalarGridSpec(
        num_scalar_prefetch=2,
        grid=(n_rows // MOE_ROWS,),
        in_specs=[pl.BlockSpec((MOE_ROWS, dm), lambda i, be, nv: (i, 0)),
                  pl.BlockSpec((1, 1, dm, hid), lambda i, be, nv: (layer, be[i], 0, 0)),
                  pl.BlockSpec((1, 1, dm, hid), lambda i, be, nv: (layer, be[i], 0, 0)),
                  pl.BlockSpec((1, 1, hid, dm), lambda i, be, nv: (layer, be[i], 0, 0))],
        out_specs=pl.BlockSpec((MOE_ROWS, dm), lambda i, be, nv: (i, 0)),
    )
    return pl.pallas_call(
        _ffn_kernel,
        grid_spec=grid_spec,
        out_shape=jax.ShapeDtypeStruct((n_rows, dm), F32),
        compiler_params=_params("arbitrary"),
        name="expert_ffn",
    )(blk_exp, n_valid, rows, wg, wu, wd)


def _combine_ln_kernel(tok0, pos_ref, x_ref, gate_ref, rows_ref, g_ref, b_ref, o_ref, buf, sem):
    tm = x_ref.shape[0]
    slot0 = (tok0 + pl.program_id(0) * tm) * MOE_TOPK
    _row_copies(pos_ref, slot0, tm, lambda r, k, src: pltpu.make_async_copy(
        rows_ref.at[pl.ds(src, 1), :], buf.at[k, pl.ds(r, 1), :], sem))
    gates = gate_ref[...]
    y = ALPHA * x_ref[...]
    for k in range(MOE_TOPK):
        y = y + gates[:, k:k + 1] * buf[k]
    o_ref[...] = _layer_norm(y, g_ref, b_ref)


def _combine_ln(pos, x, gates, rows, g, b, tok0):
    n_tok, dm = x.shape
    tm = 256
    row = pl.BlockSpec((tm, dm), lambda i, pos: (i, 0))
    vec = pl.BlockSpec((1, dm), lambda i, pos: (0, 0))
    grid_spec = pltpu.PrefetchScalarGridSpec(
        num_scalar_prefetch=1,
        grid=(n_tok // tm,),
        in_specs=[row, pl.BlockSpec((tm, HEAD_DIM), lambda i, pos: (i, 0)), pl.BlockSpec(memory_space=pl.ANY), vec, vec],
        out_specs=row,
        scratch_shapes=[pltpu.VMEM((MOE_TOPK, tm, dm), F32), pltpu.SemaphoreType.DMA(())],
    )
    return pl.pallas_call(
        functools.partial(_combine_ln_kernel, tok0),
        grid_spec=grid_spec,
        out_shape=jax.ShapeDtypeStruct((n_tok, dm), F32),
        compiler_params=_params("arbitrary"),
        name="moe_combine_ln",
    )(pos, x, gates, rows, g.reshape(1, dm), b.reshape(1, dm))


def _moe_ln(xp, xs, w_rg, b_rg, w_re, b_re, wg, wu, wd, layer, ln_g, ln_b):
    dm = xp.shape[1]
    n_grp, _, epg = w_re.shape
    n_exp = wg.shape[1]
    n_col = n_grp + n_grp * epg
    w = jnp.concatenate([w_rg, w_re.transpose(1, 0, 2).reshape(dm, n_grp * epg)], axis=1)
    w = jnp.pad(w, ((0, 0), (0, HEAD_DIM - n_col)))
    b = jnp.pad(jnp.concatenate([b_rg, b_re.reshape(-1)]), (0, HEAD_DIM - n_col)).reshape(1, HEAD_DIM)
    eid_p, gate_p = _router(xp, w, b, n_grp, epg)
    eid_s, gate_s = _router(xs, w, b, n_grp, epg)
    eid = jnp.concatenate([eid_p, eid_s], axis=0)
    rank, counts = _rank(eid)
    counts = counts[0, :n_exp].astype(jnp.int32)
    pcounts = (counts + MOE_ROWS - 1) // MOE_ROWS * MOE_ROWS
    pends = jnp.cumsum(pcounts)
    pstarts = pends - pcounts
    choice = eid[:, :MOE_TOPK]
    start_of = jnp.sum(jnp.where(choice[..., None] == jnp.arange(n_exp, dtype=jnp.int32), pstarts, 0), axis=-1)
    pos = (rank[:, :MOE_TOPK] + start_of).reshape(-1)
    n_slot = pos.shape[0]
    n_blk = -(-(n_slot + n_exp * (MOE_ROWS - 1)) // MOE_ROWS)
    blk_start = jnp.arange(n_blk, dtype=jnp.int32) * MOE_ROWS
    blk_exp = jnp.minimum(jnp.sum(blk_start[:, None] >= pends[None, :], axis=1), n_exp - 1).astype(jnp.int32)
    n_valid = (pends[-1] // MOE_ROWS).astype(jnp.int32).reshape(1)
    rows = jnp.zeros((n_blk * MOE_ROWS, dm), F32)
    rows = _dispatch(pos, xp, rows, 0)
    rows = _dispatch(pos, xs, rows, xp.shape[0])
    rows = _expert_ffn(rows, blk_exp, n_valid, wg, wu, wd, layer)
    return (_combine_ln(pos, xp, gate_p, rows, ln_g, ln_b, 0),
            _combine_ln(pos, xs, gate_s, rows, ln_g, ln_b, xp.shape[0]))


def kernel(x_prompt, x_sample, cache_kv_ab, state_kv_c_w128, state_kv_c_w512, state_kv_c_w2048, page_table, rel_bias,
           w_in_ab, w_out_ab, w_in_c, w_out_c, ln_mix_g, ln_mix_b, w_route_grp, b_route_grp, w_route_exp, b_route_exp,
           w_gate, w_up, w_down, ln_ffn_g, ln_ffn_b):
    bsz, seq, dm = x_prompt.shape
    nseq, t_len, _ = x_sample.shape
    n_pages = page_table.shape[1]
    past_len = n_pages * PAGE_SIZE
    c_states = (state_kv_c_w128, state_kv_c_w512, state_kv_c_w2048)
    xp = x_prompt.reshape(bsz * seq, dm)
    xs = x_sample.reshape(nseq * t_len, dm)
    d_ab = AB_HEADS * HEAD_DIM
    d_c = C_HEADS * HEAD_DIM
    d_cg = C_GROUP_HEADS * HEAD_DIM
    outs = {}

    for layer in range(DEPTH):
        li = layer // 2
        if layer % 2 == 0:
            w_in = w_in_ab[li].astype(BF16)
            w_out = w_out_ab[li].astype(BF16)
            qp = _matmul(xp, w_in, 0, 1, d_ab)
            kvp = _matmul(xp, w_in, d_ab // 512, 1, 2 * d_ab)
            qs = _matmul(xs, w_in, 0, 1, d_ab)
            kvs = _matmul(xs, w_in, d_ab // 512, 1, 2 * d_ab)
            att_p = _ab_prompt(qp, kvp, _moba_prompt_bias(rel_bias), bsz, seq)
            cache = cache_kv_ab[li].reshape(cache_kv_ab.shape[1], PAGE_SIZE, 4, SUBLANES, HEAD_DIM)
            tab, tabn = _moba_sample_tables(rel_bias, n_pages, t_len, past_len)
            att_s = _ab_sample(qs, kvs, cache, page_table, tab, tabn, t_len)
            mix_p = _matmul(att_p, w_out, 0, 1, dm)
            mix_s = _matmul(att_s, w_out, 0, 1, dm)
            outs["kv_ab_p"] = kvp.reshape(1, bsz, seq, 2, AB_HEADS, HEAD_DIM)
            outs["kv_ab_s"] = kvs.reshape(1, nseq, t_len, 2, AB_HEADS, HEAD_DIM)
        else:
            w_in = w_in_c[li].astype(BF16)
            w_out = w_out_c[li].astype(BF16)
            qp = _matmul(xp, w_in, 0, 1, d_c)
            qs = _matmul(xs, w_in, 0, 1, d_c)
            kvp, kvs = [], []
            for g in range(len(C_GROUPS)):
                kvp.append(_matmul(xp, w_in, d_c // 512 + g, d_c // 512, 2 * d_cg))
                kvs.append(_matmul(xs, w_in, d_c // 512 + g, d_c // 512, 2 * d_cg))
            att_p = _c_prompt(qp, kvp, _c_prompt_masks(rel_bias, seq), bsz, seq)
            q_pad = jnp.pad(qs.reshape(nseq * t_len, len(C_GROUPS), C_GROUP_HEADS, HEAD_DIM),
                            ((0, 0), (0, 0), (0, 8 - C_GROUP_HEADS), (0, 0)))
            kvn3 = [kv.reshape(nseq * t_len, 8, HEAD_DIM) for kv in kvs]
            st8 = [st[li][None].reshape(1, nseq, st.shape[2], 8, HEAD_DIM) for st in c_states]
            att_s, *next_states = _c_sample(q_pad, kvn3, st8, _c_sample_masks(rel_bias), t_len)
            att_s = att_s[:, :, :C_GROUP_HEADS].reshape(nseq * t_len, d_cg)
            mix_p = _matmul(att_p, w_out, 0, 1, dm)
            mix_s = _matmul(att_s, w_out, 0, 1, dm)
            for g, (win, _) in enumerate(C_GROUPS):
                keep = min(win, seq)
                kv5 = kvp[g].reshape(bsz, seq, 2, C_GROUP_HEADS, HEAD_DIM)
                outs[f"c_p{g}"] = kv5[:, seq - keep:][None]
                outs[f"c_s{g}"] = next_states[g].reshape((1, nseq, win, 2, C_GROUP_HEADS, HEAD_DIM))
        xp = _res_ln(xp, mix_p, ln_mix_g[layer], ln_mix_b[layer])
        xs = _res_ln(xs, mix_s, ln_mix_g[layer], ln_mix_b[layer])
        xp, xs = _moe_ln(xp, xs, w_route_grp[layer], b_route_grp[layer], w_route_exp[layer], b_route_exp[layer],
                         w_gate, w_up, w_down, layer, ln_ffn_g[layer], ln_ffn_b[layer])

    return (xp.reshape(bsz, seq, dm), xs.reshape(nseq, t_len, dm), outs["kv_ab_p"], outs["kv_ab_s"],
            outs["c_p0"], outs["c_s0"], outs["c_p1"], outs["c_s1"], outs["c_p2"], outs["c_s2"])
```

```python
import functools
import math

import jax
import jax.numpy as jnp
import numpy as np
from jax import lax
from jax.experimental import pallas as pl
from jax.experimental.pallas import tpu as pltpu

F32 = jnp.float32
BF16 = jnp.bfloat16
NEG_INF = float("-inf")
HIGHEST = lax.Precision.HIGHEST

HEAD_DIM = 128
MOBA_HEADS = 12
SB_HEADS = 4
AB_HEADS = MOBA_HEADS + SB_HEADS
MOBA_BLOCK = 256
MOBA_TOPK = 3
PAGE_SIZE = 128
C_GROUPS = ((128, 1), (512, 4), (2048, 16))
C_GROUP_HEADS = 4
C_HEADS = C_GROUP_HEADS * len(C_GROUPS)
N_BUCKETS = 32
REL_MAX_DIST = 128
MOE_TOPK = 2
DEPTH = 2
ALPHA = (2 * DEPTH) ** 0.25
LN_EPS = 1e-5
SCALE = HEAD_DIM ** -0.5

SUBLANES = 8
QBLK = 256
MOBA_STEP_BLOCKS = 4
MOE_ROWS = 256
C_SLABS = 144
PAGES_PER_STEP = 4
N_NEAR = 1
MASK_PENALTY = 2.0 ** 100
VMEM_LIMIT = 56 * 1024 * 1024


def _dot(a, b, **kw):
    return jnp.dot(a, b, preferred_element_type=F32, **kw)


def _dot_nt(a, b, **kw):
    return lax.dot_general(a, b, (((1,), (1,)), ((), ())), preferred_element_type=F32, **kw)


def _split_dot(x, u):
    hi = x.astype(BF16)
    lo = (x - hi.astype(F32)).astype(BF16)
    return _dot(hi, u) + _dot(lo, u)


def _log_sigmoid(z):
    return jnp.minimum(z, 0.0) - jnp.log(1.0 + jnp.exp(-jnp.abs(z)))


def _params(*sem):
    return pltpu.CompilerParams(dimension_semantics=sem, vmem_limit_bytes=VMEM_LIMIT)


def _bucket_np(dist):
    dist = np.asarray(dist, np.int32)
    exact = N_BUCKETS // 2
    far = np.maximum(dist, exact).astype(np.float32)
    frac = np.log(far / np.float32(exact)) / np.float32(math.log(REL_MAX_DIST / exact))
    big = exact + (frac * np.float32(N_BUCKETS - exact)).astype(np.int32)
    return np.where(dist < exact, dist, np.minimum(big, N_BUCKETS - 1)).astype(np.int32)


def _bias_lookup(bias_cols, idx):
    flat = jnp.asarray(idx.reshape(-1))
    onehot = (jnp.arange(N_BUCKETS, dtype=jnp.int32)[:, None] == flat[None, :]).astype(F32)
    return jnp.dot(bias_cols.T, onehot, precision=HIGHEST).reshape((bias_cols.shape[1],) + idx.shape)


def _topk_select(gate, valid, n_cand):
    lane = lax.broadcasted_iota(jnp.int32, gate.shape, 1)
    sels = []
    for n in range(n_cand):
        g_n = gate[:, n:n + 1]
        beats = jnp.where(valid, jnp.where((gate > g_n) | ((gate == g_n) & (lane < n)), 1.0, 0.0), 0.0)
        sels.append(jnp.sum(beats, axis=1, keepdims=True) < MOBA_TOPK)
    return sels


def _mm_kernel(x_ref, w_ref, o_ref):
    o_ref[...] = _dot(x_ref[...].astype(BF16), w_ref[...].astype(BF16))


def _matmul(x, w, col0, cstep, ncols, tn=512):
    m, k = x.shape
    tm = min(m, 1024)
    return pl.pallas_call(
        _mm_kernel,
        grid=(m // tm, ncols // tn),
        in_specs=[pl.BlockSpec((tm, k), lambda i, j: (i, 0)),
                  pl.BlockSpec((k, tn), lambda i, j: (0, col0 + j * cstep))],
        out_specs=pl.BlockSpec((tm, tn), lambda i, j: (i, j)),
        out_shape=jax.ShapeDtypeStruct((m, ncols), F32),
        compiler_params=_params("parallel", "parallel"),
        name="matmul",
    )(x, w)


def _layer_norm(y, g_ref, b_ref):
    mu = jnp.mean(y, axis=-1, keepdims=True)
    yc = y - mu
    var = jnp.mean(yc * yc, axis=-1, keepdims=True)
    return yc * lax.rsqrt(var + LN_EPS) * g_ref[...] + b_ref[...]


def _res_ln_kernel(x_ref, a_ref, g_ref, b_ref, o_ref):
    o_ref[...] = _layer_norm(ALPHA * x_ref[...] + a_ref[...], g_ref, b_ref)


def _res_ln(x, a, g, b):
    m, d = x.shape
    tm = min(m, 512)
    row = pl.BlockSpec((tm, d), lambda i: (i, 0))
    vec = pl.BlockSpec((1, d), lambda i: (0, 0))
    return pl.pallas_call(
        _res_ln_kernel,
        grid=(m // tm,),
        in_specs=[row, row, vec, vec],
        out_specs=row,
        out_shape=jax.ShapeDtypeStruct((m, d), F32),
        compiler_params=_params("parallel"),
        name="res_ln",
    )(x, a, g.reshape(1, d), b.reshape(1, d))


def _ab_prompt_kernel(q_ref, k_ref, v_ref, bias_ref, o_ref, means_ref, kx_sc, v_sc):
    h = pl.program_id(1)
    i = pl.program_id(2)
    n_blk = k_ref.shape[0] // QBLK
    row = lax.broadcasted_iota(jnp.int32, (QBLK, QBLK), 0)
    col = lax.broadcasted_iota(jnp.int32, (QBLK, QBLK), 1)
    n_steps = i // 2 + 1

    @pl.when(i == 0)
    def _stage_keys():
        seq = k_ref.shape[0]
        kx_sc[:, 0:HEAD_DIM] = k_ref[...].astype(BF16)
        blk_of_row = lax.broadcasted_iota(jnp.int32, (seq, HEAD_DIM), 0) // QBLK
        blk_of_col = lax.broadcasted_iota(jnp.int32, (seq, HEAD_DIM), 1)
        kx_sc[:, HEAD_DIM:2 * HEAD_DIM] = jnp.where(blk_of_row == blk_of_col, MASK_PENALTY, 0.0).astype(BF16)
        v_sc[...] = v_ref[...].astype(BF16)

    def kv_pair(jp):
        start = pl.multiple_of(jp * 2 * QBLK, 2 * QBLK)
        return kx_sc[pl.ds(start, 2 * QBLK), :], v_sc[pl.ds(start, 2 * QBLK), :]

    def halves(x):
        return x[:, 0:QBLK], x[:, QBLK:2 * QBLK]

    @pl.when(h < MOBA_HEADS)
    def _moba():
        @pl.when(i == 0)
        def _():
            means_ref[...] = jnp.sum(k_ref[...].reshape(n_blk, QBLK, HEAD_DIM), axis=1) * (1.0 / QBLK)

        q = q_ref[...]
        gate = _dot_nt(means_ref[...], q, precision=HIGHEST)
        blk = lax.broadcasted_iota(jnp.int32, gate.shape, 0)
        sel_t = jnp.zeros(gate.shape, F32)
        for n in range(n_blk):
            g_n = gate[n:n + 1, :]
            beats = jnp.where(blk < i, jnp.where((gate > g_n) | ((gate == g_n) & (blk < n)), 1.0, 0.0), 0.0)
            top = jnp.where(jnp.sum(beats, axis=0, keepdims=True) < MOBA_TOPK, 1.0, 0.0) * jnp.where(n < i, 1.0, 0.0)
            sel_t = jnp.where(blk == n, top + jnp.where(n == i, 1.0, 0.0), sel_t)
        sel_t = jnp.concatenate([sel_t, jnp.zeros((HEAD_DIM - n_blk, QBLK), F32)], axis=0).astype(BF16)
        sel = _dot_nt(jnp.where(row == col, 1.0, 0.0).astype(BF16), sel_t)
        drop = (sel - 1.0).astype(BF16)

        qs = jnp.concatenate([(q * SCALE).astype(BF16), drop], axis=1)

        def body(jj, carry):
            m, l, acc = carry
            jg = i // MOBA_STEP_BLOCKS - jj
            start = pl.multiple_of(jg * MOBA_STEP_BLOCKS * QBLK, MOBA_STEP_BLOCKS * QBLK)
            s = _dot_nt(qs, kx_sc[pl.ds(start, MOBA_STEP_BLOCKS * QBLK), :])
            s = s + jnp.concatenate([bias_ref[0, jnp.clip(i - MOBA_STEP_BLOCKS * jg - part, 0, 2)]
                                     for part in range(MOBA_STEP_BLOCKS)], axis=1)
            m_new = jnp.maximum(m, jnp.max(s, axis=1, keepdims=True))
            a = jnp.exp(m - m_new)
            p = jnp.exp(s - m_new)
            v_t = v_sc[pl.ds(start, MOBA_STEP_BLOCKS * QBLK), :]
            return m_new, a * l + jnp.sum(p, axis=1, keepdims=True), a * acc + _dot(p.astype(BF16), v_t)

        init = (jnp.full((QBLK, 1), NEG_INF, F32), jnp.zeros((QBLK, 1), F32), jnp.zeros((QBLK, HEAD_DIM), F32))
        m, l, acc = lax.fori_loop(0, i // MOBA_STEP_BLOCKS + 1, body, init)
        o_ref[...] = acc / l

    @pl.when(h >= MOBA_HEADS)
    def _stick_breaking():
        qs = (q_ref[...] * SCALE).astype(BF16)
        upper = jnp.where(row > col, 1.0, 0.0).astype(BF16)

        def body(jj, carry, own_pair):
            later, acc = carry
            jp = i // 2 - jj
            k_t, v_t = kv_pair(jp)
            z = _dot_nt(qs, k_t[:, 0:HEAD_DIM])
            ls = _log_sigmoid(z)
            lks = list(halves(ls - z))
            if own_pair:
                keeps = []
                for half in range(2):
                    off = i - (2 * jp + half)
                    keeps.append(jnp.where(off == 0, jnp.where(col < row, 1.0, 0.0), jnp.where(off > 0, 1.0, 0.0)) > 0.5)
                    lks[half] = jnp.where(keeps[half], lks[half], 0.0)
            inner = _split_dot(jnp.concatenate(lks, axis=0), upper)
            total = [jnp.sum(x, axis=1, keepdims=True) for x in lks]
            after = (inner[0:QBLK] + (later + total[1]), inner[QBLK:2 * QBLK] + later)
            ws = [jnp.exp(ls_h + after[half]) for half, ls_h in enumerate(halves(ls))]
            if own_pair:
                ws = [jnp.where(keeps[half], ws[half], 0.0) for half in range(2)]
            acc = acc + _dot(jnp.concatenate(ws, axis=1).astype(BF16), v_t)
            return later + total[0] + total[1], acc

        init = (jnp.zeros((QBLK, 1), F32), jnp.zeros((QBLK, HEAD_DIM), F32))
        carry = body(0, init, True)
        o_ref[...] = lax.fori_loop(1, n_steps, functools.partial(body, own_pair=False), carry)[1]


def _ab_prompt(q, kv, bias_tiles, bsz, seq):
    nq = seq // QBLK
    assert nq % MOBA_STEP_BLOCKS == 0 and nq % 2 == 0 and nq <= SUBLANES
    return pl.pallas_call(
        _ab_prompt_kernel,
        grid=(bsz, AB_HEADS, nq),
        in_specs=[pl.BlockSpec((QBLK, HEAD_DIM), lambda b, h, i: (b * nq + i, h)),
                  pl.BlockSpec((seq, HEAD_DIM), lambda b, h, i: (b, h)),
                  pl.BlockSpec((seq, HEAD_DIM), lambda b, h, i: (b, AB_HEADS + h)),
                  pl.BlockSpec((1, 3, QBLK, QBLK), lambda b, h, i: (jnp.minimum(h, MOBA_HEADS - 1), 0, 0, 0))],
        out_specs=pl.BlockSpec((QBLK, HEAD_DIM), lambda b, h, i: (b * nq + i, h)),
        out_shape=jax.ShapeDtypeStruct(q.shape, F32),
        scratch_shapes=[pltpu.VMEM((seq // QBLK, HEAD_DIM), F32),
                        pltpu.VMEM((seq, 2 * HEAD_DIM), BF16),
                        pltpu.VMEM((seq, HEAD_DIM), BF16)],
        compiler_params=_params("parallel", "parallel", "arbitrary"),
        name="ab_prompt",
    )(q, kv, kv, bias_tiles)


def _moba_prompt_bias(rel_bias):
    t = np.arange(QBLK)
    dist = np.arange(3)[:, None, None] * QBLK + t[None, :, None] - t[None, None, :]
    return jnp.where((dist >= 0)[None], _bias_lookup(rel_bias, _bucket_np(np.maximum(dist, 0))), NEG_INF)


def _ab_sample_kernel(pt_ref, q_ref, kvn_ref, *refs):
    del pt_ref
    n_pg = PAGES_PER_STEP
    page_refs = refs[:n_pg]
    tab_ref, tabn_ref, o_ref, qs_sc, qg_sc, part_sc, bsum, kscr, vscr, sb_carry, sb_acc = refs[n_pg:]
    s = pl.program_id(1)
    t_len = q_ref.shape[0]
    pages_per_block = MOBA_BLOCK // PAGE_SIZE
    n_blocks = part_sc.shape[0] - 1
    n_pages = n_blocks * pages_per_block
    n_steps = n_pages // PAGES_PER_STEP
    rows = SUBLANES * t_len
    page_rows = PAGE_SIZE * SUBLANES
    moba_rows = (rows, (MOBA_HEADS - SUBLANES) * t_len)
    eye_r = lax.broadcasted_iota(jnp.int32, (PAGE_SIZE, PAGE_SIZE), 0)
    eye_c = lax.broadcasted_iota(jnp.int32, (PAGE_SIZE, PAGE_SIZE), 1)
    upper = jnp.where(eye_r > eye_c, 1.0, 0.0).astype(BF16)

    def head_cols(ref, h):
        return ref[:, h * HEAD_DIM:(h + 1) * HEAD_DIM]

    def moba_partial(slot, hg, pieces):
        nr = moba_rows[hg]
        qs = qs_sc[hg, 0:nr]
        scores = [_dot_nt(qs, kx) + table for kx, _, table in pieces]
        m = functools.reduce(jnp.maximum, [jnp.max(sc, axis=1, keepdims=True) for sc in scores])
        es = [jnp.exp(sc - m) for sc in scores]
        part_sc[slot, hg, 0, 0:nr] = jnp.broadcast_to(m, (nr, HEAD_DIM))
        part_sc[slot, hg, 1, 0:nr] = jnp.broadcast_to(sum(jnp.sum(e, axis=1, keepdims=True) for e in es), (nr, HEAD_DIM))
        part_sc[slot, hg, 2, 0:nr] = sum(_dot(e, vx) for e, (_, vx, _) in zip(es, pieces))

    def sb_step(keys, values, causal):
        n_chunk = keys[0].shape[0] // PAGE_SIZE
        zs = []
        for j in range(SB_HEADS):
            z = _dot_nt(head_cols(q_ref, MOBA_HEADS + j) * SCALE, keys[j])
            zs += [z[:, c * PAGE_SIZE:(c + 1) * PAGE_SIZE] for c in range(n_chunk)]
        z = jnp.concatenate(zs, axis=0)
        ls = _log_sigmoid(z)
        lk = ls - z
        if causal:
            tq = lax.broadcasted_iota(jnp.int32, z.shape, 0) % t_len
            valid = lax.broadcasted_iota(jnp.int32, z.shape, 1) < tq
            lk = jnp.where(valid, lk, 0.0)
        inner = ls + _split_dot(lk, upper)
        total = jnp.sum(lk, axis=1, keepdims=True)
        for j in range(SB_HEADS):
            later = sb_carry[j]
            ws = [None] * n_chunk
            for c in reversed(range(n_chunk)):
                r0 = (j * n_chunk + c) * t_len
                w = jnp.exp(inner[r0:r0 + t_len] + later)
                ws[c] = jnp.where(valid[r0:r0 + t_len], w, 0.0) if causal else w
                later = later + total[r0:r0 + t_len]
            sb_carry[j] = later
            sb_acc[j] = sb_acc[j] + _dot(jnp.concatenate(ws, axis=1), values[j])

    @pl.when(s == 0)
    def _new_tokens():
        bsum[...] = jnp.zeros_like(bsum)
        sb_carry[...] = jnp.zeros_like(sb_carry)
        sb_acc[...] = jnp.zeros_like(sb_acc)
        pad = jnp.zeros((HEAD_DIM - rows, HEAD_DIM), F32)
        for hg in range(2):
            heads = range(hg * SUBLANES, (hg + 1) * SUBLANES)
            qst = jnp.concatenate([head_cols(q_ref, h) for h in heads], axis=0)
            qg_sc[hg] = qst
            qs_sc[hg] = qst * SCALE
            kx = jnp.concatenate([head_cols(kvn_ref, h) for h in heads] + [pad], axis=0)
            vx = jnp.concatenate([head_cols(kvn_ref, AB_HEADS + h) for h in heads] + [pad], axis=0)
            moba_partial(0, hg, [(kx, vx, tabn_ref[hg, 0:moba_rows[hg]])])
        pad = jnp.zeros((PAGE_SIZE - t_len, HEAD_DIM), F32)
        sb_step([jnp.concatenate([head_cols(kvn_ref, MOBA_HEADS + j), pad], axis=0) for j in range(SB_HEADS)],
                [jnp.concatenate([head_cols(kvn_ref, AB_HEADS + MOBA_HEADS + j), pad], axis=0) for j in range(SB_HEADS)],
                True)

    def _pages():
        page0 = n_pages - 1 - s * PAGES_PER_STEP
        for ub in range(0, PAGES_PER_STEP, pages_per_block):
            blk = (page0 - ub) // pages_per_block
            for hg in range(2):
                pieces = []
                for u in reversed(range(ub, ub + pages_per_block)):
                    near = n_pages - (page0 - u)
                    tix = jnp.where(near <= N_NEAR, near, 0)
                    k3 = page_refs[u][0, :, hg]
                    bsum[hg, blk] = bsum[hg, blk] + jnp.sum(k3, axis=0)
                    kx = k3.reshape(page_rows, HEAD_DIM)
                    vx = page_refs[u][0, :, 2 + hg].reshape(page_rows, HEAD_DIM)
                    pieces.append((kx, vx, tab_ref[tix, hg, 0:moba_rows[hg]]))
                    if hg == 1:
                        kscr[u * page_rows:(u + 1) * page_rows] = kx
                        vscr[u * page_rows:(u + 1) * page_rows] = vx
                moba_partial(blk + 1, hg, pieces)

        def sb_rows(scr, j):
            h8 = MOBA_HEADS - SUBLANES + j
            return jnp.concatenate([scr[pl.ds(u * page_rows + h8, PAGE_SIZE, stride=SUBLANES), :]
                                    for u in reversed(range(PAGES_PER_STEP))], axis=0)

        sb_step([sb_rows(kscr, j) for j in range(SB_HEADS)], [sb_rows(vscr, j) for j in range(SB_HEADS)], False)

    _pages()

    @pl.when(s == n_steps - 1)
    def _finish():
        fold = jnp.where(eye_r // SUBLANES == eye_c, 1.0, 0.0)
        for hg in range(2):
            nr = moba_rows[hg]
            lane = lax.broadcasted_iota(jnp.int32, (nr, HEAD_DIM), 1)
            rowi = lax.broadcasted_iota(jnp.int32, (nr, HEAD_DIM), 0)
            g_all = _dot_nt(qg_sc[hg, 0:nr], bsum[hg].reshape(HEAD_DIM, HEAD_DIM), precision=HIGHEST)
            g_own = jnp.where(lane % SUBLANES == rowi // t_len, g_all, 0.0)
            gate = _dot(g_own, fold, precision=HIGHEST) * (1.0 / MOBA_BLOCK)
            sels = _topk_select(gate, lane < n_blocks, n_blocks)

            def part(slot, k, hg=hg, nr=nr):
                return part_sc[slot, hg, k, 0:nr]

            m_all = part(0, 0)
            for n in range(n_blocks):
                m_all = jnp.maximum(m_all, jnp.where(sels[n], part(n + 1, 0), NEG_INF))
            wgt = jnp.exp(part(0, 0) - m_all)
            num = wgt * part(0, 2)
            den = wgt * part(0, 1)
            for n in range(n_blocks):
                wgt = jnp.where(sels[n], jnp.exp(part(n + 1, 0) - m_all), 0.0)
                num = num + wgt * part(n + 1, 2)
                den = den + wgt * part(n + 1, 1)
            out = num / den
            for h8 in range(nr // t_len):
                h = hg * SUBLANES + h8
                o_ref[:, h * HEAD_DIM:(h + 1) * HEAD_DIM] = out[h8 * t_len:(h8 + 1) * t_len]
        for j in range(SB_HEADS):
            o_ref[:, (MOBA_HEADS + j) * HEAD_DIM:(MOBA_HEADS + j + 1) * HEAD_DIM] = sb_acc[j]


def _ab_sample(q, kvn, cache, page_table, tab, tabn, t_len):
    assert AB_HEADS == 2 * SUBLANES and SUBLANES <= MOBA_HEADS and MOBA_BLOCK % PAGE_SIZE == 0
    nseq, n_pages = page_table.shape
    assert n_pages % PAGES_PER_STEP == 0 and PAGES_PER_STEP * PAGE_SIZE % MOBA_BLOCK == 0
    n_blocks = n_pages * PAGE_SIZE // MOBA_BLOCK
    n_steps = n_pages // PAGES_PER_STEP
    d_all = AB_HEADS * HEAD_DIM
    rows = SUBLANES * t_len

    def page_map(b, s, pt, *, u):
        page = n_pages - 1 - (s * PAGES_PER_STEP + u)
        return (pt[b * n_pages + page], 0, 0, 0, 0)

    page_specs = [pl.BlockSpec((1, PAGE_SIZE, 4, SUBLANES, HEAD_DIM), functools.partial(page_map, u=u))
                  for u in range(PAGES_PER_STEP)]
    grid_spec = pltpu.PrefetchScalarGridSpec(
        num_scalar_prefetch=1,
        grid=(nseq, n_steps),
        in_specs=[pl.BlockSpec((t_len, d_all), lambda b, s, pt: (b, 0)),
                  pl.BlockSpec((t_len, 2 * d_all), lambda b, s, pt: (b, 0))] + page_specs +
                 [pl.BlockSpec(tab.shape, lambda b, s, pt: (0, 0, 0, 0)),
                  pl.BlockSpec(tabn.shape, lambda b, s, pt: (0, 0, 0))],
        out_specs=pl.BlockSpec((t_len, d_all), lambda b, s, pt: (b, 0)),
        scratch_shapes=[pltpu.VMEM((2, rows, HEAD_DIM), F32),
                        pltpu.VMEM((2, rows, HEAD_DIM), F32),
                        pltpu.VMEM((n_blocks + 1, 2, 3, rows, HEAD_DIM), F32),
                        pltpu.VMEM((2, HEAD_DIM // SUBLANES, SUBLANES, HEAD_DIM), F32),
                        pltpu.VMEM((PAGES_PER_STEP * PAGE_SIZE * SUBLANES, HEAD_DIM), F32),
                        pltpu.VMEM((PAGES_PER_STEP * PAGE_SIZE * SUBLANES, HEAD_DIM), F32),
                        pltpu.VMEM((SB_HEADS, t_len, HEAD_DIM), F32),
                        pltpu.VMEM((SB_HEADS, t_len, HEAD_DIM), F32)],
    )
    return pl.pallas_call(
        _ab_sample_kernel,
        grid_spec=grid_spec,
        out_shape=jax.ShapeDtypeStruct((nseq * t_len, d_all), F32),
        compiler_params=_params("parallel", "arbitrary"),
        name="ab_sample",
    )(page_table.reshape(-1), q, kvn, *([cache] * PAGES_PER_STEP), tab, tabn)


def _moba_sample_tables(rel_bias, n_pages, t_len, past_len):
    assert N_NEAR * PAGE_SIZE + 1 >= REL_MAX_DIST and past_len == n_pages * PAGE_SIZE
    t = np.arange(t_len)
    pos = np.arange(PAGE_SIZE)
    dist = [np.full((t_len, PAGE_SIZE), REL_MAX_DIST)]
    for near in range(1, N_NEAR + 1):
        dist.append(past_len + t[:, None] - ((n_pages - near) * PAGE_SIZE + pos[None, :]))
    idx = _bucket_np(np.maximum(np.stack(dist), 0))
    n_tab = idx.shape[0]
    same = np.eye(SUBLANES, dtype=bool)
    pad_heads = ((0, AB_HEADS - MOBA_HEADS),)
    bias = jnp.pad(_bias_lookup(rel_bias, idx), pad_heads + ((0, 0),) * 3)
    bias = bias.reshape(2, SUBLANES, n_tab, t_len, PAGE_SIZE).transpose(2, 0, 1, 3, 4)
    tab = jnp.where(same[None, None, :, None, None, :], bias[..., None], NEG_INF)
    tab = tab.reshape(n_tab, 2, SUBLANES * t_len, PAGE_SIZE * SUBLANES)
    idx_n = _bucket_np(np.maximum(t[:, None] - t[None, :], 0))
    bias_n = jnp.pad(_bias_lookup(rel_bias, idx_n), pad_heads + ((0, 0),) * 2).reshape(2, SUBLANES, t_len, t_len)
    ok = same[:, None, :, None] & (t[None, :, None, None] >= t[None, None, None, :])
    tabn = jnp.where(ok[None], bias_n[:, :, :, None, :], NEG_INF).reshape(2, SUBLANES * t_len, SUBLANES * t_len)
    tabn = jnp.pad(tabn, ((0, 0), (0, 0), (0, HEAD_DIM - SUBLANES * t_len)), constant_values=NEG_INF)
    return tab, tabn


def _c_tile_counts(seq):
    return tuple(min(win // QBLK + 1, seq // QBLK) if win >= QBLK else 2 for win, _ in C_GROUPS)


def _c_prompt_kernel(counts, *refs):
    q_refs, kv_refs, (mask_ref, o_ref) = refs[0:3], refs[3:9], refs[9:]
    i = pl.program_id(2)
    n_q = kv_refs[0].shape[0] // QBLK
    n_steps = i // 2 + 1
    qs = [(q_ref[...] * SCALE).astype(BF16) for q_ref in q_refs]
    tile0 = [sum(counts[:g]) + g for g in range(len(counts))]

    def step(jj, carry, groups):
        m, l, acc = carry
        jp = i // 2 - jj
        start = pl.multiple_of(jp * 2 * QBLK, 2 * QBLK)
        scores, values = [], []
        for g in groups:
            k_t = kv_refs[2 * g][pl.ds(start, 2 * QBLK), :].astype(BF16)
            values.append(kv_refs[2 * g + 1][pl.ds(start, 2 * QBLK), :].astype(BF16))
            tiles = []
            for half in range(2):
                off = i - (2 * jp + half)
                tiles.append(mask_ref[0, tile0[g] + jnp.where((off >= 0) & (off < counts[g]), off, counts[g])])
            scores.append(_dot_nt(qs[g], k_t) + jnp.concatenate(tiles, axis=1))
        m_new = functools.reduce(jnp.maximum, [m] + [jnp.max(sc, axis=1, keepdims=True) for sc in scores])
        a = jnp.exp(m - m_new)
        ps = [jnp.exp(sc - m_new) for sc in scores]
        l = a * l + sum(jnp.sum(p, axis=1, keepdims=True) for p in ps)
        acc = a * acc + sum(_dot(p.astype(BF16), v_t) for p, v_t in zip(ps, values))
        return m_new, l, acc

    every = tuple(range(len(counts)))
    wide = tuple(g for g in every if counts[g] == n_q)
    short_steps = max((counts[g] + 2) // 2 for g in every if g not in wide)
    state = (jnp.full((QBLK, 1), NEG_INF, F32), jnp.zeros((QBLK, 1), F32), jnp.zeros((QBLK, HEAD_DIM), F32))
    state = lax.fori_loop(0, jnp.minimum(n_steps, short_steps), functools.partial(step, groups=every), state)
    if wide:
        state = lax.fori_loop(short_steps, n_steps, functools.partial(step, groups=wide), state)
    m, l, acc = state
    o_ref[...] = acc / l


def _c_prompt(q, kvs, mask_tiles, bsz, seq):
    nq = seq // QBLK
    counts = _c_tile_counts(seq)
    q_specs = [pl.BlockSpec((QBLK, HEAD_DIM), lambda b, j, i, g=g: (b * nq + i, C_GROUP_HEADS * g + j))
               for g in range(len(C_GROUPS))]
    kv_specs = []
    for g in range(len(C_GROUPS)):
        kv_specs.append(pl.BlockSpec((seq, HEAD_DIM), lambda b, j, i: (b, j)))
        kv_specs.append(pl.BlockSpec((seq, HEAD_DIM), lambda b, j, i: (b, C_GROUP_HEADS + j)))
    kv_args = [kvs[g] for g in range(len(C_GROUPS)) for _ in range(2)]
    return pl.pallas_call(
        functools.partial(_c_prompt_kernel, counts),
        grid=(bsz, C_GROUP_HEADS, nq),
        in_specs=q_specs + kv_specs + [pl.BlockSpec((1,) + mask_tiles.shape[1:], lambda b, j, i: (j, 0, 0, 0))],
        out_specs=pl.BlockSpec((QBLK, HEAD_DIM), lambda b, j, i: (b * nq + i, j)),
        out_shape=jax.ShapeDtypeStruct((bsz * seq, C_GROUP_HEADS * HEAD_DIM), F32),
        compiler_params=_params("parallel", "parallel", "arbitrary"),
        name="c_prompt",
    )(q, q, q, *kv_args, mask_tiles)


def _c_prompt_masks(rel_bias, seq):
    t = np.arange(QBLK)
    tiles = []
    for g, ((win, dil), n_tiles) in enumerate(zip(C_GROUPS, _c_tile_counts(seq))):
        d = np.arange(n_tiles)[:, None, None] * QBLK + t[None, :, None] - t[None, None, :]
        ok = (d >= 0) & (d <= win) & (d % dil == 0)
        bias = _bias_lookup(rel_bias[:, C_GROUP_HEADS * g:C_GROUP_HEADS * (g + 1)], _bucket_np(np.maximum(d, 0)))
        tiles.append(jnp.where(ok[None], bias, NEG_INF))
        tiles.append(jnp.full((C_GROUP_HEADS, 1, QBLK, QBLK), NEG_INF, F32))
    return jnp.concatenate(tiles, axis=1)


def _c_sample_kernel(q_ref, *refs):
    n_g = len(C_GROUPS)
    kvn_refs, st_refs = refs[0:n_g], refs[n_g:2 * n_g]
    mask_ref, o_ref = refs[2 * n_g], refs[2 * n_g + 1]
    next_refs = refs[2 * n_g + 2:]
    t_len = q_ref.shape[0]
    for g in range(n_g):
        n_buf = st_refs[g].shape[2]
        next_refs[g][0, 0, 0:n_buf - t_len] = st_refs[g][0, 0, t_len:n_buf]
        next_refs[g][0, 0, n_buf - t_len:n_buf] = kvn_refs[g][...]
    for t in range(t_len):
        xs, scores = [], []
        for g, (win, dil) in enumerate(C_GROUPS):
            n_buf = st_refs[g].shape[2]
            m_max = win // dil
            first = n_buf + t - m_max * dil
            n_old = -(-(n_buf - first) // dil)
            old = st_refs[g][0, 0, pl.ds(first, n_old, stride=dil)]
            new = [kvn_refs[g][first + c * dil - n_buf][None] for c in range(n_old, m_max + 1)]
            pad = jnp.zeros((C_SLABS - m_max - 1, 8, HEAD_DIM), F32)
            x = jnp.concatenate([old] + new + [pad], axis=0).reshape(C_SLABS * 8, HEAD_DIM)
            xs.append(x)
            scores.append(_dot_nt(q_ref[t, g] * SCALE, x) + mask_ref[g])
        m = functools.reduce(jnp.maximum, [jnp.max(sc, axis=1, keepdims=True) for sc in scores])
        ps = [jnp.exp(sc - m) for sc in scores]
        l = sum(jnp.sum(p, axis=1, keepdims=True) for p in ps)
        acc = sum(_dot(pltpu.roll(p, C_GROUP_HEADS, axis=1), x) for p, x in zip(ps, xs))
        o_ref[0, t] = acc / l


def _c_sample(q, kvns, states, masks, t_len):
    nseq = states[0].shape[1]
    n_g = len(C_GROUPS)
    in_specs = [pl.BlockSpec((t_len, n_g, 8, HEAD_DIM), lambda b: (b, 0, 0, 0))]
    in_specs += [pl.BlockSpec((t_len, 8, HEAD_DIM), lambda b: (b, 0, 0)) for _ in range(n_g)]
    in_specs += [pl.BlockSpec((1, 1) + st.shape[2:], lambda b: (0, b, 0, 0, 0)) for st in states]
    in_specs += [pl.BlockSpec(masks.shape, lambda b: (0, 0, 0))]
    for st, (win, dil) in zip(states, C_GROUPS):
        assert st.shape[2] == win and win // dil < C_SLABS
    st_specs = [pl.BlockSpec((1, 1) + st.shape[2:], lambda b: (0, b, 0, 0, 0)) for st in states]
    return pl.pallas_call(
        _c_sample_kernel,
        grid=(nseq,),
        in_specs=in_specs,
        out_specs=[pl.BlockSpec((1, t_len, 8, HEAD_DIM), lambda b: (b, 0, 0, 0))] + st_specs,
        out_shape=[jax.ShapeDtypeStruct((nseq, t_len, 8, HEAD_DIM), F32)] +
                  [jax.ShapeDtypeStruct(st.shape, F32) for st in states],
        compiler_params=_params("parallel"),
        name="c_sample",
    )(q, *kvns, *states, masks)


def _c_sample_masks(rel_bias):
    c = np.arange(C_SLABS * 8)
    slab, is_v, head = c // 8, (c % 8) // C_GROUP_HEADS, c % C_GROUP_HEADS
    slot = np.arange(8)
    out = []
    for g, (win, dil) in enumerate(C_GROUPS):
        m_max = win // dil
        steps = m_max - slab
        ok = (steps >= 0)[None, :] & (is_v == 0)[None, :] & (head[None, :] == slot[:, None])
        bias = _bias_lookup(rel_bias[:, C_GROUP_HEADS * g:C_GROUP_HEADS * (g + 1)],
                            _bucket_np(np.maximum(steps, 0) * dil))
        bias = jnp.concatenate([bias, jnp.zeros_like(bias)], axis=0)
        out.append(jnp.where(slot[:, None] < C_GROUP_HEADS, jnp.where(ok, bias, NEG_INF), 0.0))
    return jnp.stack(out)


def _router_kernel(n_grp, epg, x_ref, w_ref, b_ref, e_ref, g_ref):
    logits = _dot(x_ref[...], w_ref[...], precision=HIGHEST) + b_ref[...]
    lane = lax.broadcasted_iota(jnp.int32, logits.shape, 1)
    big = jnp.int32(2 ** 30)

    def first_max(valid):
        v = jnp.max(jnp.where(valid, logits, NEG_INF), axis=1, keepdims=True)
        idx = jnp.min(jnp.where(valid & (logits == v), lane, big), axis=1, keepdims=True)
        return v, idx

    in_grp = lane < n_grp
    g_max, grp = first_max(in_grp)
    p_grp = 1.0 / jnp.sum(jnp.where(in_grp, jnp.exp(logits - g_max), 0.0), axis=1, keepdims=True)
    lo = n_grp + grp * epg
    in_exp = (lane >= lo) & (lane < lo + epg)
    v1, i1 = first_max(in_exp)
    v2, i2 = first_max(in_exp & (lane != i1))
    e2 = jnp.exp(v2 - v1)
    g1 = p_grp / (1.0 + e2)
    g2 = p_grp * e2 / (1.0 + e2)
    e_ref[...] = jnp.where(lane == 0, i1 - n_grp, jnp.where(lane == 1, i2 - n_grp, 0))
    g_ref[...] = jnp.where(lane == 0, g1, jnp.where(lane == 1, g2, 0.0))


def _router(x, w, b, n_grp, epg):
    n_tok, dm = x.shape
    tm = min(n_tok, 512)
    row = pl.BlockSpec((tm, HEAD_DIM), lambda i: (i, 0))
    return pl.pallas_call(
        functools.partial(_router_kernel, n_grp, epg),
        grid=(n_tok // tm,),
        in_specs=[pl.BlockSpec((tm, dm), lambda i: (i, 0)),
                  pl.BlockSpec((dm, HEAD_DIM), lambda i: (0, 0)),
                  pl.BlockSpec((1, HEAD_DIM), lambda i: (0, 0))],
        out_specs=[row, row],
        out_shape=[jax.ShapeDtypeStruct((n_tok, HEAD_DIM), jnp.int32), jax.ShapeDtypeStruct((n_tok, HEAD_DIM), F32)],
        compiler_params=_params("parallel"),
        name="router",
    )(x, w, b)


def _rank_kernel(e_ref, r_ref, c_ref, carry):
    @pl.when(pl.program_id(0) == 0)
    def _():
        carry[...] = jnp.zeros_like(carry)

    e = e_ref[...]
    tm = e.shape[0]
    lane = lax.broadcasted_iota(jnp.int32, e.shape, 1)
    hits = [lane == e[:, k:k + 1] for k in range(MOE_TOPK)]
    both = sum(jnp.where(hit, 1.0, 0.0) for hit in hits)
    earlier = jnp.where(lax.broadcasted_iota(jnp.int32, (tm, tm), 0) > lax.broadcasted_iota(jnp.int32, (tm, tm), 1),
                        1.0, 0.0).astype(BF16)
    prefix = _dot(earlier, both.astype(BF16)) + carry[0:1, :]
    rank = jnp.zeros(e.shape, F32)
    for k, hit in enumerate(hits):
        r_k = jnp.sum(jnp.where(hit, prefix, 0.0), axis=1, keepdims=True)
        if k:
            r_k = r_k + sum(jnp.where(e[:, k:k + 1] == e[:, kk:kk + 1], 1.0, 0.0) for kk in range(k))
        rank = jnp.where(lane == k, r_k, rank)
    r_ref[...] = rank.astype(jnp.int32)
    carry[...] = carry[...] + jnp.sum(both, axis=0, keepdims=True)
    c_ref[...] = carry[...]


def _rank(eid):
    n_tok = eid.shape[0]
    tm = 512
    row = pl.BlockSpec((tm, HEAD_DIM), lambda i: (i, 0))
    return pl.pallas_call(
        _rank_kernel,
        grid=(n_tok // tm,),
        in_specs=[row],
        out_specs=[row, pl.BlockSpec((SUBLANES, HEAD_DIM), lambda i: (0, 0))],
        out_shape=[jax.ShapeDtypeStruct((n_tok, HEAD_DIM), jnp.int32), jax.ShapeDtypeStruct((SUBLANES, HEAD_DIM), F32)],
        scratch_shapes=[pltpu.VMEM((SUBLANES, HEAD_DIM), F32)],
        compiler_params=_params("arbitrary"),
        name="moe_rank",
    )(eid)


def _row_copies(pos_ref, slot0, tm, make_copy):
    def copy(r, k):
        return make_copy(r, k, pos_ref[slot0 + r * MOE_TOPK + k])

    def issue(r, c):
        for k in range(MOE_TOPK):
            copy(r, k).start()
        return c

    def drain(r, c):
        for k in range(MOE_TOPK):
            copy(r, k).wait()
        return c

    lax.fori_loop(0, tm, issue, 0, unroll=4)
    lax.fori_loop(0, tm, drain, 0, unroll=4)


def _dispatch_kernel(tok0, pos_ref, x_ref, rows_in, rows_out, sem):
    del rows_in
    tm = x_ref.shape[0]
    slot0 = (tok0 + pl.program_id(0) * tm) * MOE_TOPK
    _row_copies(pos_ref, slot0, tm, lambda r, k, dst: pltpu.make_async_copy(
        x_ref.at[pl.ds(r, 1), :], rows_out.at[pl.ds(dst, 1), :], sem))


def _dispatch(pos, x, rows, tok0):
    n_tok, dm = x.shape
    tm = min(n_tok, 512)
    grid_spec = pltpu.PrefetchScalarGridSpec(
        num_scalar_prefetch=1,
        grid=(n_tok // tm,),
        in_specs=[pl.BlockSpec((tm, dm), lambda i, pos: (i, 0)), pl.BlockSpec(memory_space=pl.ANY)],
        out_specs=pl.BlockSpec(memory_space=pl.ANY),
        scratch_shapes=[pltpu.SemaphoreType.DMA(())],
    )
    return pl.pallas_call(
        functools.partial(_dispatch_kernel, tok0),
        grid_spec=grid_spec,
        out_shape=jax.ShapeDtypeStruct(rows.shape, F32),
        input_output_aliases={2: 0},
        compiler_params=_params("arbitrary"),
        name="moe_dispatch",
    )(pos, x, rows)


def _ffn_kernel(be_ref, nv_ref, x_ref, wg_ref, wu_ref, wd_ref, o_ref):
    del be_ref

    @pl.when(pl.program_id(0) < nv_ref[0])
    def _():
        xb = x_ref[...].astype(BF16)
        g = _dot(xb, wg_ref[0, 0].astype(BF16))
        u = _dot(xb, wu_ref[0, 0].astype(BF16))
        hid = g * (1.0 / (1.0 + jnp.exp(-g))) * u
        o_ref[...] = _dot(hid.astype(BF16), wd_ref[0, 0].astype(BF16))

    @pl.when(pl.program_id(0) >= nv_ref[0])
    def _():
        o_ref[...] = jnp.zeros_like(o_ref)


def _expert_ffn(rows, blk_exp, n_valid, wg, wu, wd, layer):
    n_rows, dm = rows.shape
    hid = wg.shape[3]
    grid_spec = pltpu.PrefetchScalarGridSpec(
        num_scalar_prefetch=2,
        grid=(n_rows // MOE_ROWS,),
        in_specs=[pl.BlockSpec((MOE_ROWS, dm), lambda i, be, nv: (i, 0)),
                  pl.BlockSpec((1, 1, dm, hid), lambda i, be, nv: (layer, be[i], 0, 0)),
                  pl.BlockSpec((1, 1, dm, hid), lambda i, be, nv: (layer, be[i], 0, 0)),
                  pl.BlockSpec((1, 1, hid, dm), lambda i, be, nv: (layer, be[i], 0, 0))],
        out_specs=pl.BlockSpec((MOE_ROWS, dm), lambda i, be, nv: (i, 0)),
    )
    return pl.pallas_call(
        _ffn_kernel,
        grid_spec=grid_spec,
        out_shape=jax.ShapeDtypeStruct((n_rows, dm), F32),
        compiler_params=_params("arbitrary"),
        name="expert_ffn",
    )(blk_exp, n_valid, rows, wg, wu, wd)


def _combine_ln_kernel(tok0, pos_ref, x_ref, gate_ref, rows_ref, g_ref, b_ref, o_ref, buf, sem):
    tm = x_ref.shape[0]
    slot0 = (tok0 + pl.program_id(0) * tm) * MOE_TOPK
    _row_copies(pos_ref, slot0, tm, lambda r, k, src: pltpu.make_async_copy(
        rows_ref.at[pl.ds(src, 1), :], buf.at[k, pl.ds(r, 1), :], sem))
    gates = gate_ref[...]
    y = ALPHA * x_ref[...]
    for k in range(MOE_TOPK):
        y = y + gates[:, k:k + 1] * buf[k]
    o_ref[...] = _layer_norm(y, g_ref, b_ref)


def _combine_ln(pos, x, gates, rows, g, b, tok0):
    n_tok, dm = x.shape
    tm = 256
    row = pl.BlockSpec((tm, dm), lambda i, pos: (i, 0))
    vec = pl.BlockSpec((1, dm), lambda i, pos: (0, 0))
    grid_spec = pltpu.PrefetchScalarGridSpec(
        num_scalar_prefetch=1,
        grid=(n_tok // tm,),
        in_specs=[row, pl.BlockSpec((tm, HEAD_DIM), lambda i, pos: (i, 0)), pl.BlockSpec(memory_space=pl.ANY), vec, vec],
        out_specs=row,
        scratch_shapes=[pltpu.VMEM((MOE_TOPK, tm, dm), F32), pltpu.SemaphoreType.DMA(())],
    )
    return pl.pallas_call(
        functools.partial(_combine_ln_kernel, tok0),
        grid_spec=grid_spec,
        out_shape=jax.ShapeDtypeStruct((n_tok, dm), F32),
        compiler_params=_params("arbitrary"),
        name="moe_combine_ln",
    )(pos, x, gates, rows, g.reshape(1, dm), b.reshape(1, dm))


def _moe_ln(xp, xs, w_rg, b_rg, w_re, b_re, wg, wu, wd, layer, ln_g, ln_b):
    dm = xp.shape[1]
    n_grp, _, epg = w_re.shape
    n_exp = wg.shape[1]
    n_col = n_grp + n_grp * epg
    w = jnp.concatenate([w_rg, w_re.transpose(1, 0, 2).reshape(dm, n_grp * epg)], axis=1)
    w = jnp.pad(w, ((0, 0), (0, HEAD_DIM - n_col)))
    b = jnp.pad(jnp.concatenate([b_rg, b_re.reshape(-1)]), (0, HEAD_DIM - n_col)).reshape(1, HEAD_DIM)
    eid_p, gate_p = _router(xp, w, b, n_grp, epg)
    eid_s, gate_s = _router(xs, w, b, n_grp, epg)
    eid = jnp.concatenate([eid_p, eid_s], axis=0)
    rank, counts = _rank(eid)
    counts = counts[0, :n_exp].astype(jnp.int32)
    pcounts = (counts + MOE_ROWS - 1) // MOE_ROWS * MOE_ROWS
    pends = jnp.cumsum(pcounts)
    pstarts = pends - pcounts
    choice = eid[:, :MOE_TOPK]
    start_of = jnp.sum(jnp.where(choice[..., None] == jnp.arange(n_exp, dtype=jnp.int32), pstarts, 0), axis=-1)
    pos = (rank[:, :MOE_TOPK] + start_of).reshape(-1)
    n_slot = pos.shape[0]
    n_blk = -(-(n_slot + n_exp * (MOE_ROWS - 1)) // MOE_ROWS)
    blk_start = jnp.arange(n_blk, dtype=jnp.int32) * MOE_ROWS
    blk_exp = jnp.minimum(jnp.sum(blk_start[:, None] >= pends[None, :], axis=1), n_exp - 1).astype(jnp.int32)
    n_valid = (pends[-1] // MOE_ROWS).astype(jnp.int32).reshape(1)
    rows = jnp.zeros((n_blk * MOE_ROWS, dm), F32)
    rows = _dispatch(pos, xp, rows, 0)
    rows = _dispatch(pos, xs, rows, xp.shape[0])
    rows = _expert_ffn(rows, blk_exp, n_valid, wg, wu, wd, layer)
    return (_combine_ln(pos, xp, gate_p, rows, ln_g, ln_b, 0),
            _combine_ln(pos, xs, gate_s, rows, ln_g, ln_b, xp.shape[0]))


def kernel(x_prompt, x_sample, cache_kv_ab, state_kv_c_w128, state_kv_c_w512, state_kv_c_w2048, page_table, rel_bias,
           w_in_ab, w_out_ab, w_in_c, w_out_c, ln_mix_g, ln_mix_b, w_route_grp, b_route_grp, w_route_exp, b_route_exp,
           w_gate, w_up, w_down, ln_ffn_g, ln_ffn_b):
    bsz, seq, dm = x_prompt.shape
    nseq, t_len, _ = x_sample.shape
    n_pages = page_table.shape[1]
    past_len = n_pages * PAGE_SIZE
    c_states = (state_kv_c_w128, state_kv_c_w512, state_kv_c_w2048)
    xp = x_prompt.reshape(bsz * seq, dm)
    xs = x_sample.reshape(nseq * t_len, dm)
    d_ab = AB_HEADS * HEAD_DIM
    d_c = C_HEADS * HEAD_DIM
    d_cg = C_GROUP_HEADS * HEAD_DIM
    outs = {}

    for layer in range(DEPTH):
        li = layer // 2
        if layer % 2 == 0:
            w_in = w_in_ab[li].astype(BF16)
            w_out = w_out_ab[li].astype(BF16)
            qp = _matmul(xp, w_in, 0, 1, d_ab)
            kvp = _matmul(xp, w_in, d_ab // 512, 1, 2 * d_ab)
            qs = _matmul(xs, w_in, 0, 1, d_ab)
            kvs = _matmul(xs, w_in, d_ab // 512, 1, 2 * d_ab)
            att_p = _ab_prompt(qp, kvp, _moba_prompt_bias(rel_bias), bsz, seq)
            cache = cache_kv_ab[li].reshape(cache_kv_ab.shape[1], PAGE_SIZE, 4, SUBLANES, HEAD_DIM)
            tab, tabn = _moba_sample_tables(rel_bias, n_pages, t_len, past_len)
            att_s = _ab_sample(qs, kvs, cache, page_table, tab, tabn, t_len)
            mix_p = _matmul(att_p, w_out, 0, 1, dm)
            mix_s = _matmul(att_s, w_out, 0, 1, dm)
            outs["kv_ab_p"] = kvp.reshape(1, bsz, seq, 2, AB_HEADS, HEAD_DIM)
            outs["kv_ab_s"] = kvs.reshape(1, nseq, t_len, 2, AB_HEADS, HEAD_DIM)
        else:
            w_in = w_in_c[li].astype(BF16)
            w_out = w_out_c[li].astype(BF16)
            qp = _matmul(xp, w_in, 0, 1, d_c)
            qs = _matmul(xs, w_in, 0, 1, d_c)
            kvp, kvs = [], []
            for g in range(len(C_GROUPS)):
                kvp.append(_matmul(xp, w_in, d_c // 512 + g, d_c // 512, 2 * d_cg))
                kvs.append(_matmul(xs, w_in, d_c // 512 + g, d_c // 512, 2 * d_cg))
            att_p = _c_prompt(qp, kvp, _c_prompt_masks(rel_bias, seq), bsz, seq)
            q_pad = jnp.pad(qs.reshape(nseq * t_len, len(C_GROUPS), C_GROUP_HEADS, HEAD_DIM),
                            ((0, 0), (0, 0), (0, 8 - C_GROUP_HEADS), (0, 0)))
            kvn3 = [kv.reshape(nseq * t_len, 8, HEAD_DIM) for kv in kvs]
            st8 = [st[li][None].reshape(1, nseq, st.shape[2], 8, HEAD_DIM) for st in c_states]
            att_s, *next_states = _c_sample(q_pad, kvn3, st8, _c_sample_masks(rel_bias), t_len)
            att_s = att_s[:, :, :C_GROUP_HEADS].reshape(nseq * t_len, d_cg)
            mix_p = _matmul(att_p, w_out, 0, 1, dm)
            mix_s = _matmul(att_s, w_out, 0, 1, dm)
            for g, (win, _) in enumerate(C_GROUPS):
                keep = min(win, seq)
                kv5 = kvp[g].reshape(bsz, seq, 2, C_GROUP_HEADS, HEAD_DIM)
                outs[f"c_p{g}"] = kv5[:, seq - keep:][None]
                outs[f"c_s{g}"] = next_states[g].reshape((1, nseq, win, 2, C_GROUP_HEADS, HEAD_DIM))
        xp = _res_ln(xp, mix_p, ln_mix_g[layer], ln_mix_b[layer])
        xs = _res_ln(xs, mix_s, ln_mix_g[layer], ln_mix_b[layer])
        xp, xs = _moe_ln(xp, xs, w_route_grp[layer], b_route_grp[layer], w_route_exp[layer], b_route_exp[layer],
                         w_gate, w_up, w_down, layer, ln_ffn_g[layer], ln_ffn_b[layer])

    return (xp.reshape(bsz, seq, dm), xs.reshape(nseq, t_len, dm), outs["kv_ab_p"], outs["kv_ab_s"],
            outs["c_p0"], outs["c_s0"], outs["c_p1"], outs["c_s1"], outs["c_p2"], outs["c_s2"])
```

```python
import functools
import math

import jax
import jax.numpy as jnp
import numpy as np
from jax import lax
from jax.experimental import pallas as pl
from jax.experimental.pallas import tpu as pltpu

F32 = jnp.float32
BF16 = jnp.bfloat16
NEG_INF = float("-inf")
HIGHEST = lax.Precision.HIGHEST

HEAD_DIM = 128
MOBA_HEADS = 12
SB_HEADS = 4
AB_HEADS = MOBA_HEADS + SB_HEADS
MOBA_BLOCK = 256
MOBA_TOPK = 3
PAGE_SIZE = 128
C_GROUPS = ((128, 1), (512, 4), (2048, 16))
C_GROUP_HEADS = 4
C_HEADS = C_GROUP_HEADS * len(C_GROUPS)
N_BUCKETS = 32
REL_MAX_DIST = 128
MOE_TOPK = 2
DEPTH = 2
ALPHA = (2 * DEPTH) ** 0.25
LN_EPS = 1e-5
SCALE = HEAD_DIM ** -0.5

SUBLANES = 8
QBLK = 256
MOBA_STEP_BLOCKS = 4
MOE_ROWS = 256
C_SLABS = 144
PAGES_PER_STEP = 8
N_NEAR = 1
MASK_PENALTY = 2.0 ** 100
VMEM_LIMIT = 56 * 1024 * 1024


def _dot(a, b, **kw):
    return jnp.dot(a, b, preferred_element_type=F32, **kw)


def _dot_nt(a, b, **kw):
    return lax.dot_general(a, b, (((1,), (1,)), ((), ())), preferred_element_type=F32, **kw)


def _split_dot(x, u):
    hi = x.astype(BF16)
    lo = (x - hi.astype(F32)).astype(BF16)
    return _dot(hi, u) + _dot(lo, u)


def _log_sigmoid(z):
    return jnp.minimum(z, 0.0) - jnp.log(1.0 + jnp.exp(-jnp.abs(z)))


def _params(*sem):
    return pltpu.CompilerParams(dimension_semantics=sem, vmem_limit_bytes=VMEM_LIMIT)


def _bucket_np(dist):
    dist = np.asarray(dist, np.int32)
    exact = N_BUCKETS // 2
    far = np.maximum(dist, exact).astype(np.float32)
    frac = np.log(far / np.float32(exact)) / np.float32(math.log(REL_MAX_DIST / exact))
    big = exact + (frac * np.float32(N_BUCKETS - exact)).astype(np.int32)
    return np.where(dist < exact, dist, np.minimum(big, N_BUCKETS - 1)).astype(np.int32)


def _bias_lookup(bias_cols, idx):
    flat = jnp.asarray(idx.reshape(-1))
    onehot = (jnp.arange(N_BUCKETS, dtype=jnp.int32)[:, None] == flat[None, :]).astype(F32)
    return jnp.dot(bias_cols.T, onehot, precision=HIGHEST).reshape((bias_cols.shape[1],) + idx.shape)


def _topk_select(gate, valid, n_cand):
    lane = lax.broadcasted_iota(jnp.int32, gate.shape, 1)
    sels = []
    for n in range(n_cand):
        g_n = gate[:, n:n + 1]
        beats = jnp.where(valid, jnp.where((gate > g_n) | ((gate == g_n) & (lane < n)), 1.0, 0.0), 0.0)
        sels.append(jnp.sum(beats, axis=1, keepdims=True) < MOBA_TOPK)
    return sels


def _mm_kernel(x_ref, w_ref, o_ref):
    o_ref[...] = _dot(x_ref[...].astype(BF16), w_ref[...].astype(BF16))


def _matmul(x, w, col0, cstep, ncols, tn=512):
    m, k = x.shape
    tm = min(m, 1024)
    return pl.pallas_call(
        _mm_kernel,
        grid=(m // tm, ncols // tn),
        in_specs=[pl.BlockSpec((tm, k), lambda i, j: (i, 0)),
                  pl.BlockSpec((k, tn), lambda i, j: (0, col0 + j * cstep))],
        out_specs=pl.BlockSpec((tm, tn), lambda i, j: (i, j)),
        out_shape=jax.ShapeDtypeStruct((m, ncols), F32),
        compiler_params=_params("parallel", "parallel"),
        name="matmul",
    )(x, w)


def _layer_norm(y, g_ref, b_ref):
    mu = jnp.mean(y, axis=-1, keepdims=True)
    yc = y - mu
    var = jnp.mean(yc * yc, axis=-1, keepdims=True)
    return yc * lax.rsqrt(var + LN_EPS) * g_ref[...] + b_ref[...]


def _res_ln_kernel(x_ref, a_ref, g_ref, b_ref, o_ref):
    o_ref[...] = _layer_norm(ALPHA * x_ref[...] + a_ref[...], g_ref, b_ref)


def _res_ln(x, a, g, b):
    m, d = x.shape
    tm = min(m, 512)
    row = pl.BlockSpec((tm, d), lambda i: (i, 0))
    vec = pl.BlockSpec((1, d), lambda i: (0, 0))
    return pl.pallas_call(
        _res_ln_kernel,
        grid=(m // tm,),
        in_specs=[row, row, vec, vec],
        out_specs=row,
        out_shape=jax.ShapeDtypeStruct((m, d), F32),
        compiler_params=_params("parallel"),
        name="res_ln",
    )(x, a, g.reshape(1, d), b.reshape(1, d))


def _ab_prompt_kernel(q_ref, k_ref, v_ref, bias_ref, o_ref, means_ref, kx_sc, v_sc):
    h = pl.program_id(1)
    i = pl.program_id(2)
    n_blk = k_ref.shape[0] // QBLK
    row = lax.broadcasted_iota(jnp.int32, (QBLK, QBLK), 0)
    col = lax.broadcasted_iota(jnp.int32, (QBLK, QBLK), 1)
    n_steps = i // 2 + 1

    @pl.when(i == 0)
    def _stage_keys():
        seq = k_ref.shape[0]
        kx_sc[:, 0:HEAD_DIM] = k_ref[...].astype(BF16)
        blk_of_row = lax.broadcasted_iota(jnp.int32, (seq, HEAD_DIM), 0) // QBLK
        blk_of_col = lax.broadcasted_iota(jnp.int32, (seq, HEAD_DIM), 1)
        kx_sc[:, HEAD_DIM:2 * HEAD_DIM] = jnp.where(blk_of_row == blk_of_col, MASK_PENALTY, 0.0).astype(BF16)
        v_sc[...] = v_ref[...].astype(BF16)

    def kv_pair(jp):
        start = pl.multiple_of(jp * 2 * QBLK, 2 * QBLK)
        return kx_sc[pl.ds(start, 2 * QBLK), :], v_sc[pl.ds(start, 2 * QBLK), :]

    def halves(x):
        return x[:, 0:QBLK], x[:, QBLK:2 * QBLK]

    @pl.when(h < MOBA_HEADS)
    def _moba():
        @pl.when(i == 0)
        def _():
            means_ref[...] = jnp.sum(k_ref[...].reshape(n_blk, QBLK, HEAD_DIM), axis=1) * (1.0 / QBLK)

        q = q_ref[...]
        gate = _dot_nt(means_ref[...], q, precision=HIGHEST)
        blk = lax.broadcasted_iota(jnp.int32, gate.shape, 0)
        sel_t = jnp.zeros(gate.shape, F32)
        for n in range(n_blk):
            g_n = gate[n:n + 1, :]
            beats = jnp.where(blk < i, jnp.where((gate > g_n) | ((gate == g_n) & (blk < n)), 1.0, 0.0), 0.0)
            top = jnp.where(jnp.sum(beats, axis=0, keepdims=True) < MOBA_TOPK, 1.0, 0.0) * jnp.where(n < i, 1.0, 0.0)
            sel_t = jnp.where(blk == n, top + jnp.where(n == i, 1.0, 0.0), sel_t)
        sel_t = jnp.concatenate([sel_t, jnp.zeros((HEAD_DIM - n_blk, QBLK), F32)], axis=0).astype(BF16)
        sel = _dot_nt(jnp.where(row == col, 1.0, 0.0).astype(BF16), sel_t)
        drop = (sel - 1.0).astype(BF16)

        qs = jnp.concatenate([(q * SCALE).astype(BF16), drop], axis=1)

        def body(jj, carry):
            m, l, acc = carry
            jg = i // MOBA_STEP_BLOCKS - jj
            start = pl.multiple_of(jg * MOBA_STEP_BLOCKS * QBLK, MOBA_STEP_BLOCKS * QBLK)
            s = _dot_nt(qs, kx_sc[pl.ds(start, MOBA_STEP_BLOCKS * QBLK), :])
            s = s + jnp.concatenate([bias_ref[0, jnp.clip(i - MOBA_STEP_BLOCKS * jg - part, 0, 2)]
                                     for part in range(MOBA_STEP_BLOCKS)], axis=1)
            m_new = jnp.maximum(m, jnp.max(s, axis=1, keepdims=True))
            a = jnp.exp(m - m_new)
            p = jnp.exp(s - m_new)
            v_t = v_sc[pl.ds(start, MOBA_STEP_BLOCKS * QBLK), :]
            return m_new, a * l + jnp.sum(p, axis=1, keepdims=True), a * acc + _dot(p.astype(BF16), v_t)

        init = (jnp.full((QBLK, 1), NEG_INF, F32), jnp.zeros((QBLK, 1), F32), jnp.zeros((QBLK, HEAD_DIM), F32))
        m, l, acc = lax.fori_loop(0, i // MOBA_STEP_BLOCKS + 1, body, init)
        o_ref[...] = acc / l

    @pl.when(h >= MOBA_HEADS)
    def _stick_breaking():
        qs = (q_ref[...] * SCALE).astype(BF16)
        upper = jnp.where(row > col, 1.0, 0.0).astype(BF16)

        def body(jj, carry, own_pair):
            later, acc = carry
            jp = i // 2 - jj
            k_t, v_t = kv_pair(jp)
            z = _dot_nt(qs, k_t[:, 0:HEAD_DIM])
            ls = _log_sigmoid(z)
            lks = list(halves(ls - z))
            if own_pair:
                keeps = []
                for half in range(2):
                    off = i - (2 * jp + half)
                    keeps.append(jnp.where(off == 0, jnp.where(col < row, 1.0, 0.0), jnp.where(off > 0, 1.0, 0.0)) > 0.5)
                    lks[half] = jnp.where(keeps[half], lks[half], 0.0)
            inner = _split_dot(jnp.concatenate(lks, axis=0), upper)
            total = [jnp.sum(x, axis=1, keepdims=True) for x in lks]
            after = (inner[0:QBLK] + (later + total[1]), inner[QBLK:2 * QBLK] + later)
            ws = [jnp.exp(ls_h + after[half]) for half, ls_h in enumerate(halves(ls))]
            if own_pair:
                ws = [jnp.where(keeps[half], ws[half], 0.0) for half in range(2)]
            acc = acc + _dot(jnp.concatenate(ws, axis=1).astype(BF16), v_t)
            return later + total[0] + total[1], acc

        init = (jnp.zeros((QBLK, 1), F32), jnp.zeros((QBLK, HEAD_DIM), F32))
        carry = body(0, init, True)
        o_ref[...] = lax.fori_loop(1, n_steps, functools.partial(body, own_pair=False), carry)[1]


def _ab_prompt(q, kv, bias_tiles, bsz, seq):
    nq = seq // QBLK
    assert nq % MOBA_STEP_BLOCKS == 0 and nq % 2 == 0 and nq <= SUBLANES
    return pl.pallas_call(
        _ab_prompt_kernel,
        grid=(bsz, AB_HEADS, nq),
        in_specs=[pl.BlockSpec((QBLK, HEAD_DIM), lambda b, h, i: (b * nq + i, h)),
                  pl.BlockSpec((seq, HEAD_DIM), lambda b, h, i: (b, h)),
                  pl.BlockSpec((seq, HEAD_DIM), lambda b, h, i: (b, AB_HEADS + h)),
                  pl.BlockSpec((1, 3, QBLK, QBLK), lambda b, h, i: (jnp.minimum(h, MOBA_HEADS - 1), 0, 0, 0))],
        out_specs=pl.BlockSpec((QBLK, HEAD_DIM), lambda b, h, i: (b * nq + i, h)),
        out_shape=jax.ShapeDtypeStruct(q.shape, F32),
        scratch_shapes=[pltpu.VMEM((seq // QBLK, HEAD_DIM), F32),
                        pltpu.VMEM((seq, 2 * HEAD_DIM), BF16),
                        pltpu.VMEM((seq, HEAD_DIM), BF16)],
        compiler_params=_params("parallel", "parallel", "arbitrary"),
        name="ab_prompt",
    )(q, kv, kv, bias_tiles)


def _moba_prompt_bias(rel_bias):
    t = np.arange(QBLK)
    dist = np.arange(3)[:, None, None] * QBLK + t[None, :, None] - t[None, None, :]
    return jnp.where((dist >= 0)[None], _bias_lookup(rel_bias, _bucket_np(np.maximum(dist, 0))), NEG_INF)


def _ab_sample_kernel(pt_ref, q_ref, kvn_ref, *refs):
    del pt_ref
    n_pg = PAGES_PER_STEP
    page_refs = refs[:n_pg]
    tab_ref, tabn_ref, o_ref, qs_sc, qg_sc, part_sc, bsum, kscr, vscr, sb_carry, sb_acc = refs[n_pg:]
    s = pl.program_id(1)
    t_len = q_ref.shape[0]
    pages_per_block = MOBA_BLOCK // PAGE_SIZE
    n_blocks = part_sc.shape[0] - 1
    n_pages = n_blocks * pages_per_block
    n_steps = n_pages // PAGES_PER_STEP
    rows = SUBLANES * t_len
    page_rows = PAGE_SIZE * SUBLANES
    moba_rows = (rows, (MOBA_HEADS - SUBLANES) * t_len)
    eye_r = lax.broadcasted_iota(jnp.int32, (PAGE_SIZE, PAGE_SIZE), 0)
    eye_c = lax.broadcasted_iota(jnp.int32, (PAGE_SIZE, PAGE_SIZE), 1)
    upper = jnp.where(eye_r > eye_c, 1.0, 0.0).astype(BF16)

    def head_cols(ref, h):
        return ref[:, h * HEAD_DIM:(h + 1) * HEAD_DIM]

    def moba_partial(slot, hg, pieces):
        nr = moba_rows[hg]
        qs = qs_sc[hg, 0:nr]
        scores = [_dot_nt(qs, kx) + table for kx, _, table in pieces]
        m = functools.reduce(jnp.maximum, [jnp.max(sc, axis=1, keepdims=True) for sc in scores])
        es = [jnp.exp(sc - m) for sc in scores]
        part_sc[slot, hg, 0, 0:nr] = jnp.broadcast_to(m, (nr, HEAD_DIM))
        part_sc[slot, hg, 1, 0:nr] = jnp.broadcast_to(sum(jnp.sum(e, axis=1, keepdims=True) for e in es), (nr, HEAD_DIM))
        part_sc[slot, hg, 2, 0:nr] = sum(_dot(e, vx) for e, (_, vx, _) in zip(es, pieces))

    def sb_step(keys, values, causal):
        n_chunk = keys[0].shape[0] // PAGE_SIZE
        zs = []
        for j in range(SB_HEADS):
            z = _dot_nt(head_cols(q_ref, MOBA_HEADS + j) * SCALE, keys[j])
            zs += [z[:, c * PAGE_SIZE:(c + 1) * PAGE_SIZE] for c in range(n_chunk)]
        z = jnp.concatenate(zs, axis=0)
        ls = _log_sigmoid(z)
        lk = ls - z
        if causal:
            tq = lax.broadcasted_iota(jnp.int32, z.shape, 0) % t_len
            valid = lax.broadcasted_iota(jnp.int32, z.shape, 1) < tq
            lk = jnp.where(valid, lk, 0.0)
        inner = ls + _split_dot(lk, upper)
        total = jnp.sum(lk, axis=1, keepdims=True)
        for j in range(SB_HEADS):
            later = sb_carry[j]
            ws = [None] * n_chunk
            for c in reversed(range(n_chunk)):
                r0 = (j * n_chunk + c) * t_len
                w = jnp.exp(inner[r0:r0 + t_len] + later)
                ws[c] = jnp.where(valid[r0:r0 + t_len], w, 0.0) if causal else w
                later = later + total[r0:r0 + t_len]
            sb_carry[j] = later
            sb_acc[j] = sb_acc[j] + _dot(jnp.concatenate(ws, axis=1), values[j])

    @pl.when(s == 0)
    def _new_tokens():
        bsum[...] = jnp.zeros_like(bsum)
        sb_carry[...] = jnp.zeros_like(sb_carry)
        sb_acc[...] = jnp.zeros_like(sb_acc)
        pad = jnp.zeros((HEAD_DIM - rows, HEAD_DIM), F32)
        for hg in range(2):
            heads = range(hg * SUBLANES, (hg + 1) * SUBLANES)
            qst = jnp.concatenate([head_cols(q_ref, h) for h in heads], axis=0)
            qg_sc[hg] = qst
            qs_sc[hg] = qst * SCALE
            kx = jnp.concatenate([head_cols(kvn_ref, h) for h in heads] + [pad], axis=0)
            vx = jnp.concatenate([head_cols(kvn_ref, AB_HEADS + h) for h in heads] + [pad], axis=0)
            moba_partial(0, hg, [(kx, vx, tabn_ref[hg, 0:moba_rows[hg]])])
        pad = jnp.zeros((PAGE_SIZE - t_len, HEAD_DIM), F32)
        sb_step([jnp.concatenate([head_cols(kvn_ref, MOBA_HEADS + j), pad], axis=0) for j in range(SB_HEADS)],
                [jnp.concatenate([head_cols(kvn_ref, AB_HEADS + MOBA_HEADS + j), pad], axis=0) for j in range(SB_HEADS)],
                True)

    def _pages():
        page0 = n_pages - 1 - s * PAGES_PER_STEP
        for ub in range(0, PAGES_PER_STEP, pages_per_block):
            blk = (page0 - ub) // pages_per_block
            for hg in range(2):
                pieces = []
                for u in reversed(range(ub, ub + pages_per_block)):
                    near = n_pages - (page0 - u)
                    tix = jnp.where(near <= N_NEAR, near, 0)
                    k3 = page_refs[u][0, :, hg]
                    bsum[hg, blk] = bsum[hg, blk] + jnp.sum(k3, axis=0)
                    kx = k3.reshape(page_rows, HEAD_DIM)
                    vx = page_refs[u][0, :, 2 + hg].reshape(page_rows, HEAD_DIM)
                    pieces.append((kx, vx, tab_ref[tix, hg, 0:moba_rows[hg]]))
                    if hg == 1:
                        kscr[u * page_rows:(u + 1) * page_rows] = kx
                        vscr[u * page_rows:(u + 1) * page_rows] = vx
                moba_partial(blk + 1, hg, pieces)

        def sb_rows(scr, j):
            h8 = MOBA_HEADS - SUBLANES + j
            return jnp.concatenate([scr[pl.ds(u * page_rows + h8, PAGE_SIZE, stride=SUBLANES), :]
                                    for u in reversed(range(PAGES_PER_STEP))], axis=0)

        sb_step([sb_rows(kscr, j) for j in range(SB_HEADS)], [sb_rows(vscr, j) for j in range(SB_HEADS)], False)

    _pages()

    @pl.when(s == n_steps - 1)
    def _finish():
        fold = jnp.where(eye_r // SUBLANES == eye_c, 1.0, 0.0)
        for hg in range(2):
            nr = moba_rows[hg]
            lane = lax.broadcasted_iota(jnp.int32, (nr, HEAD_DIM), 1)
            rowi = lax.broadcasted_iota(jnp.int32, (nr, HEAD_DIM), 0)
            g_all = _dot_nt(qg_sc[hg, 0:nr], bsum[hg].reshape(HEAD_DIM, HEAD_DIM), precision=HIGHEST)
            g_own = jnp.where(lane % SUBLANES == rowi // t_len, g_all, 0.0)
            gate = _dot(g_own, fold, precision=HIGHEST) * (1.0 / MOBA_BLOCK)
            sels = _topk_select(gate, lane < n_blocks, n_blocks)

            def part(slot, k, hg=hg, nr=nr):
                return part_sc[slot, hg, k, 0:nr]

            m_all = part(0, 0)
            for n in range(n_blocks):
                m_all = jnp.maximum(m_all, jnp.where(sels[n], part(n + 1, 0), NEG_INF))
            wgt = jnp.exp(part(0, 0) - m_all)
            num = wgt * part(0, 2)
            den = wgt * part(0, 1)
            for n in range(n_blocks):
                wgt = jnp.where(sels[n], jnp.exp(part(n + 1, 0) - m_all), 0.0)
                num = num + wgt * part(n + 1, 2)
                den = den + wgt * part(n + 1, 1)
            out = num / den
            for h8 in range(nr // t_len):
                h = hg * SUBLANES + h8
                o_ref[:, h * HEAD_DIM:(h + 1) * HEAD_DIM] = out[h8 * t_len:(h8 + 1) * t_len]
        for j in range(SB_HEADS):
            o_ref[:, (MOBA_HEADS + j) * HEAD_DIM:(MOBA_HEADS + j + 1) * HEAD_DIM] = sb_acc[j]


def _ab_sample(q, kvn, cache, page_table, tab, tabn, t_len):
    assert AB_HEADS == 2 * SUBLANES and SUBLANES <= MOBA_HEADS and MOBA_BLOCK % PAGE_SIZE == 0
    nseq, n_pages = page_table.shape
    assert n_pages % PAGES_PER_STEP == 0 and PAGES_PER_STEP * PAGE_SIZE % MOBA_BLOCK == 0
    n_blocks = n_pages * PAGE_SIZE // MOBA_BLOCK
    n_steps = n_pages // PAGES_PER_STEP
    d_all = AB_HEADS * HEAD_DIM
    rows = SUBLANES * t_len

    def page_map(b, s, pt, *, u):
        page = n_pages - 1 - (s * PAGES_PER_STEP + u)
        return (pt[b * n_pages + page], 0, 0, 0, 0)

    page_specs = [pl.BlockSpec((1, PAGE_SIZE, 4, SUBLANES, HEAD_DIM), functools.partial(page_map, u=u))
                  for u in range(PAGES_PER_STEP)]
    grid_spec = pltpu.PrefetchScalarGridSpec(
        num_scalar_prefetch=1,
        grid=(nseq, n_steps),
        in_specs=[pl.BlockSpec((t_len, d_all), lambda b, s, pt: (b, 0)),
                  pl.BlockSpec((t_len, 2 * d_all), lambda b, s, pt: (b, 0))] + page_specs +
                 [pl.BlockSpec(tab.shape, lambda b, s, pt: (0, 0, 0, 0)),
                  pl.BlockSpec(tabn.shape, lambda b, s, pt: (0, 0, 0))],
        out_specs=pl.BlockSpec((t_len, d_all), lambda b, s, pt: (b, 0)),
        scratch_shapes=[pltpu.VMEM((2, rows, HEAD_DIM), F32),
                        pltpu.VMEM((2, rows, HEAD_DIM), F32),
                        pltpu.VMEM((n_blocks + 1, 2, 3, rows, HEAD_DIM), F32),
                        pltpu.VMEM((2, HEAD_DIM // SUBLANES, SUBLANES, HEAD_DIM), F32),
                        pltpu.VMEM((PAGES_PER_STEP * PAGE_SIZE * SUBLANES, HEAD_DIM), F32),
                        pltpu.VMEM((PAGES_PER_STEP * PAGE_SIZE * SUBLANES, HEAD_DIM), F32),
                        pltpu.VMEM((SB_HEADS, t_len, HEAD_DIM), F32),
                        pltpu.VMEM((SB_HEADS, t_len, HEAD_DIM), F32)],
    )
    return pl.pallas_call(
        _ab_sample_kernel,
        grid_spec=grid_spec,
        out_shape=jax.ShapeDtypeStruct((nseq * t_len, d_all), F32),
        compiler_params=_params("parallel", "arbitrary"),
        name="ab_sample",
    )(page_table.reshape(-1), q, kvn, *([cache] * PAGES_PER_STEP), tab, tabn)


def _moba_sample_tables(rel_bias, n_pages, t_len, past_len):
    assert N_NEAR * PAGE_SIZE + 1 >= REL_MAX_DIST and past_len == n_pages * PAGE_SIZE
    t = np.arange(t_len)
    pos = np.arange(PAGE_SIZE)
    dist = [np.full((t_len, PAGE_SIZE), REL_MAX_DIST)]
    for near in range(1, N_NEAR + 1):
        dist.append(past_len + t[:, None] - ((n_pages - near) * PAGE_SIZE + pos[None, :]))
    idx = _bucket_np(np.maximum(np.stack(dist), 0))
    n_tab = idx.shape[0]
    same = np.eye(SUBLANES, dtype=bool)
    pad_heads = ((0, AB_HEADS - MOBA_HEADS),)
    bias = jnp.pad(_bias_lookup(rel_bias, idx), pad_heads + ((0, 0),) * 3)
    bias = bias.reshape(2, SUBLANES, n_tab, t_len, PAGE_SIZE).transpose(2, 0, 1, 3, 4)
    tab = jnp.where(same[None, None, :, None, None, :], bias[..., None], NEG_INF)
    tab = tab.reshape(n_tab, 2, SUBLANES * t_len, PAGE_SIZE * SUBLANES)
    idx_n = _bucket_np(np.maximum(t[:, None] - t[None, :], 0))
    bias_n = jnp.pad(_bias_lookup(rel_bias, idx_n), pad_heads + ((0, 0),) * 2).reshape(2, SUBLANES, t_len, t_len)
    ok = same[:, None, :, None] & (t[None, :, None, None] >= t[None, None, None, :])
    tabn = jnp.where(ok[None], bias_n[:, :, :, None, :], NEG_INF).reshape(2, SUBLANES * t_len, SUBLANES * t_len)
    tabn = jnp.pad(tabn, ((0, 0), (0, 0), (0, HEAD_DIM - SUBLANES * t_len)), constant_values=NEG_INF)
    return tab, tabn


def _c_tile_counts(seq):
    return tuple(min(win // QBLK + 1, seq // QBLK) if win >= QBLK else 2 for win, _ in C_GROUPS)


def _c_prompt_kernel(counts, *refs):
    q_refs, kv_refs, (mask_ref, o_ref) = refs[0:3], refs[3:9], refs[9:]
    i = pl.program_id(2)
    n_q = kv_refs[0].shape[0] // QBLK
    n_steps = i // 2 + 1
    qs = [(q_ref[...] * SCALE).astype(BF16) for q_ref in q_refs]
    tile0 = [sum(counts[:g]) + g for g in range(len(counts))]

    def step(jj, carry, groups):
        m, l, acc = carry
        jp = i // 2 - jj
        start = pl.multiple_of(jp * 2 * QBLK, 2 * QBLK)
        scores, values = [], []
        for g in groups:
            k_t = kv_refs[2 * g][pl.ds(start, 2 * QBLK), :].astype(BF16)
            values.append(kv_refs[2 * g + 1][pl.ds(start, 2 * QBLK), :].astype(BF16))
            tiles = []
            for half in range(2):
                off = i - (2 * jp + half)
                tiles.append(mask_ref[0, tile0[g] + jnp.where((off >= 0) & (off < counts[g]), off, counts[g])])
            scores.append(_dot_nt(qs[g], k_t) + jnp.concatenate(tiles, axis=1))
        m_new = functools.reduce(jnp.maximum, [m] + [jnp.max(sc, axis=1, keepdims=True) for sc in scores])
        a = jnp.exp(m - m_new)
        ps = [jnp.exp(sc - m_new) for sc in scores]
        l = a * l + sum(jnp.sum(p, axis=1, keepdims=True) for p in ps)
        acc = a * acc + sum(_dot(p.astype(BF16), v_t) for p, v_t in zip(ps, values))
        return m_new, l, acc

    every = tuple(range(len(counts)))
    wide = tuple(g for g in every if counts[g] == n_q)
    short_steps = max((counts[g] + 2) // 2 for g in every if g not in wide)
    state = (jnp.full((QBLK, 1), NEG_INF, F32), jnp.zeros((QBLK, 1), F32), jnp.zeros((QBLK, HEAD_DIM), F32))
    state = lax.fori_loop(0, jnp.minimum(n_steps, short_steps), functools.partial(step, groups=every), state)
    if wide:
        state = lax.fori_loop(short_steps, n_steps, functools.partial(step, groups=wide), state)
    m, l, acc = state
    o_ref[...] = acc / l


def _c_prompt(q, kvs, mask_tiles, bsz, seq):
    nq = seq // QBLK
    counts = _c_tile_counts(seq)
    q_specs = [pl.BlockSpec((QBLK, HEAD_DIM), lambda b, j, i, g=g: (b * nq + i, C_GROUP_HEADS * g + j))
               for g in range(len(C_GROUPS))]
    kv_specs = []
    for g in range(len(C_GROUPS)):
        kv_specs.append(pl.BlockSpec((seq, HEAD_DIM), lambda b, j, i: (b, j)))
        kv_specs.append(pl.BlockSpec((seq, HEAD_DIM), lambda b, j, i: (b, C_GROUP_HEADS + j)))
    kv_args = [kvs[g] for g in range(len(C_GROUPS)) for _ in range(2)]
    return pl.pallas_call(
        functools.partial(_c_prompt_kernel, counts),
        grid=(bsz, C_GROUP_HEADS, nq),
        in_specs=q_specs + kv_specs + [pl.BlockSpec((1,) + mask_tiles.shape[1:], lambda b, j, i: (j, 0, 0, 0))],
        out_specs=pl.BlockSpec((QBLK, HEAD_DIM), lambda b, j, i: (b * nq + i, j)),
        out_shape=jax.ShapeDtypeStruct((bsz * seq, C_GROUP_HEADS * HEAD_DIM), F32),
        compiler_params=_params("parallel", "parallel", "arbitrary"),
        name="c_prompt",
    )(q, q, q, *kv_args, mask_tiles)


def _c_prompt_masks(rel_bias, seq):
    t = np.arange(QBLK)
    tiles = []
    for g, ((win, dil), n_tiles) in enumerate(zip(C_GROUPS, _c_tile_counts(seq))):
        d = np.arange(n_tiles)[:, None, None] * QBLK + t[None, :, None] - t[None, None, :]
        ok = (d >= 0) & (d <= win) & (d % dil == 0)
        bias = _bias_lookup(rel_bias[:, C_GROUP_HEADS * g:C_GROUP_HEADS * (g + 1)], _bucket_np(np.maximum(d, 0)))
        tiles.append(jnp.where(ok[None], bias, NEG_INF))
        tiles.append(jnp.full((C_GROUP_HEADS, 1, QBLK, QBLK), NEG_INF, F32))
    return jnp.concatenate(tiles, axis=1)


def _c_sample_kernel(q_ref, *refs):
    n_g = len(C_GROUPS)
    kvn_refs, st_refs = refs[0:n_g], refs[n_g:2 * n_g]
    mask_ref, o_ref = refs[2 * n_g], refs[2 * n_g + 1]
    next_refs = refs[2 * n_g + 2:]
    t_len = q_ref.shape[0]
    for g in range(n_g):
        n_buf = st_refs[g].shape[2]
        next_refs[g][0, 0, 0:n_buf - t_len] = st_refs[g][0, 0, t_len:n_buf]
        next_refs[g][0, 0, n_buf - t_len:n_buf] = kvn_refs[g][...]
    for t in range(t_len):
        xs, scores = [], []
        for g, (win, dil) in enumerate(C_GROUPS):
            n_buf = st_refs[g].shape[2]
            m_max = win // dil
            first = n_buf + t - m_max * dil
            n_old = -(-(n_buf - first) // dil)
            old = st_refs[g][0, 0, pl.ds(first, n_old, stride=dil)]
            new = [kvn_refs[g][first + c * dil - n_buf][None] for c in range(n_old, m_max + 1)]
            pad = jnp.zeros((C_SLABS - m_max - 1, 8, HEAD_DIM), F32)
            x = jnp.concatenate([old] + new + [pad], axis=0).reshape(C_SLABS * 8, HEAD_DIM)
            xs.append(x)
            scores.append(_dot_nt(q_ref[t, g] * SCALE, x) + mask_ref[g])
        m = functools.reduce(jnp.maximum, [jnp.max(sc, axis=1, keepdims=True) for sc in scores])
        ps = [jnp.exp(sc - m) for sc in scores]
        l = sum(jnp.sum(p, axis=1, keepdims=True) for p in ps)
        acc = sum(_dot(pltpu.roll(p, C_GROUP_HEADS, axis=1), x) for p, x in zip(ps, xs))
        o_ref[0, t] = acc / l


def _c_sample(q, kvns, states, masks, t_len):
    nseq = states[0].shape[1]
    n_g = len(C_GROUPS)
    in_specs = [pl.BlockSpec((t_len, n_g, 8, HEAD_DIM), lambda b: (b, 0, 0, 0))]
    in_specs += [pl.BlockSpec((t_len, 8, HEAD_DIM), lambda b: (b, 0, 0)) for _ in range(n_g)]
    in_specs += [pl.BlockSpec((1, 1) + st.shape[2:], lambda b: (0, b, 0, 0, 0)) for st in states]
    in_specs += [pl.BlockSpec(masks.shape, lambda b: (0, 0, 0))]
    for st, (win, dil) in zip(states, C_GROUPS):
        assert st.shape[2] == win and win // dil < C_SLABS
    st_specs = [pl.BlockSpec((1, 1) + st.shape[2:], lambda b: (0, b, 0, 0, 0)) for st in states]
    return pl.pallas_call(
        _c_sample_kernel,
        grid=(nseq,),
        in_specs=in_specs,
        out_specs=[pl.BlockSpec((1, t_len, 8, HEAD_DIM), lambda b: (b, 0, 0, 0))] + st_specs,
        out_shape=[jax.ShapeDtypeStruct((nseq, t_len, 8, HEAD_DIM), F32)] +
                  [jax.ShapeDtypeStruct(st.shape, F32) for st in states],
        compiler_params=_params("parallel"),
        name="c_sample",
    )(q, *kvns, *states, masks)


def _c_sample_masks(rel_bias):
    c = np.arange(C_SLABS * 8)
    slab, is_v, head = c // 8, (c % 8) // C_GROUP_HEADS, c % C_GROUP_HEADS
    slot = np.arange(8)
    out = []
    for g, (win, dil) in enumerate(C_GROUPS):
        m_max = win // dil
        steps = m_max - slab
        ok = (steps >= 0)[None, :] & (is_v == 0)[None, :] & (head[None, :] == slot[:, None])
        bias = _bias_lookup(rel_bias[:, C_GROUP_HEADS * g:C_GROUP_HEADS * (g + 1)],
                            _bucket_np(np.maximum(steps, 0) * dil))
        bias = jnp.concatenate([bias, jnp.zeros_like(bias)], axis=0)
        out.append(jnp.where(slot[:, None] < C_GROUP_HEADS, jnp.where(ok, bias, NEG_INF), 0.0))
    return jnp.stack(out)


def _router_kernel(n_grp, epg, x_ref, w_ref, b_ref, e_ref, g_ref):
    logits = _dot(x_ref[...], w_ref[...], precision=HIGHEST) + b_ref[...]
    lane = lax.broadcasted_iota(jnp.int32, logits.shape, 1)
    big = jnp.int32(2 ** 30)

    def first_max(valid):
        v = jnp.max(jnp.where(valid, logits, NEG_INF), axis=1, keepdims=True)
        idx = jnp.min(jnp.where(valid & (logits == v), lane, big), axis=1, keepdims=True)
        return v, idx

    in_grp = lane < n_grp
    g_max, grp = first_max(in_grp)
    p_grp = 1.0 / jnp.sum(jnp.where(in_grp, jnp.exp(logits - g_max), 0.0), axis=1, keepdims=True)
    lo = n_grp + grp * epg
    in_exp = (lane >= lo) & (lane < lo + epg)
    v1, i1 = first_max(in_exp)
    v2, i2 = first_max(in_exp & (lane != i1))
    e2 = jnp.exp(v2 - v1)
    g1 = p_grp / (1.0 + e2)
    g2 = p_grp * e2 / (1.0 + e2)
    e_ref[...] = jnp.where(lane == 0, i1 - n_grp, jnp.where(lane == 1, i2 - n_grp, 0))
    g_ref[...] = jnp.where(lane == 0, g1, jnp.where(lane == 1, g2, 0.0))


def _router(x, w, b, n_grp, epg):
    n_tok, dm = x.shape
    tm = min(n_tok, 512)
    row = pl.BlockSpec((tm, HEAD_DIM), lambda i: (i, 0))
    return pl.pallas_call(
        functools.partial(_router_kernel, n_grp, epg),
        grid=(n_tok // tm,),
        in_specs=[pl.BlockSpec((tm, dm), lambda i: (i, 0)),
                  pl.BlockSpec((dm, HEAD_DIM), lambda i: (0, 0)),
                  pl.BlockSpec((1, HEAD_DIM), lambda i: (0, 0))],
        out_specs=[row, row],
        out_shape=[jax.ShapeDtypeStruct((n_tok, HEAD_DIM), jnp.int32), jax.ShapeDtypeStruct((n_tok, HEAD_DIM), F32)],
        compiler_params=_params("parallel"),
        name="router",
    )(x, w, b)


def _rank_kernel(e_ref, r_ref, c_ref, carry):
    @pl.when(pl.program_id(0) == 0)
    def _():
        carry[...] = jnp.zeros_like(carry)

    e = e_ref[...]
    tm = e.shape[0]
    lane = lax.broadcasted_iota(jnp.int32, e.shape, 1)
    hits = [lane == e[:, k:k + 1] for k in range(MOE_TOPK)]
    both = sum(jnp.where(hit, 1.0, 0.0) for hit in hits)
    earlier = jnp.where(lax.broadcasted_iota(jnp.int32, (tm, tm), 0) > lax.broadcasted_iota(jnp.int32, (tm, tm), 1),
                        1.0, 0.0).astype(BF16)
    prefix = _dot(earlier, both.astype(BF16)) + carry[0:1, :]
    rank = jnp.zeros(e.shape, F32)
    for k, hit in enumerate(hits):
        r_k = jnp.sum(jnp.where(hit, prefix, 0.0), axis=1, keepdims=True)
        if k:
            r_k = r_k + sum(jnp.where(e[:, k:k + 1] == e[:, kk:kk + 1], 1.0, 0.0) for kk in range(k))
        rank = jnp.where(lane == k, r_k, rank)
    r_ref[...] = rank.astype(jnp.int32)
    carry[...] = carry[...] + jnp.sum(both, axis=0, keepdims=True)
    c_ref[...] = carry[...]


def _rank(eid):
    n_tok = eid.shape[0]
    tm = 512
    row = pl.BlockSpec((tm, HEAD_DIM), lambda i: (i, 0))
    return pl.pallas_call(
        _rank_kernel,
        grid=(n_tok // tm,),
        in_specs=[row],
        out_specs=[row, pl.BlockSpec((SUBLANES, HEAD_DIM), lambda i: (0, 0))],
        out_shape=[jax.ShapeDtypeStruct((n_tok, HEAD_DIM), jnp.int32), jax.ShapeDtypeStruct((SUBLANES, HEAD_DIM), F32)],
        scratch_shapes=[pltpu.VMEM((SUBLANES, HEAD_DIM), F32)],
        compiler_params=_params("arbitrary"),
        name="moe_rank",
    )(eid)


def _row_copies(pos_ref, slot0, tm, make_copy):
    def copy(r, k):
        return make_copy(r, k, pos_ref[slot0 + r * MOE_TOPK + k])

    def issue(r, c):
        for k in range(MOE_TOPK):
            copy(r, k).start()
        return c

    def drain(r, c):
        for k in range(MOE_TOPK):
            copy(r, k).wait()
        return c

    lax.fori_loop(0, tm, issue, 0, unroll=4)
    lax.fori_loop(0, tm, drain, 0, unroll=4)


def _dispatch_kernel(tok0, pos_ref, x_ref, rows_in, rows_out, sem):
    del rows_in
    tm = x_ref.shape[0]
    slot0 = (tok0 + pl.program_id(0) * tm) * MOE_TOPK
    _row_copies(pos_ref, slot0, tm, lambda r, k, dst: pltpu.make_async_copy(
        x_ref.at[pl.ds(r, 1), :], rows_out.at[pl.ds(dst, 1), :], sem))


def _dispatch(pos, x, rows, tok0):
    n_tok, dm = x.shape
    tm = min(n_tok, 512)
    grid_spec = pltpu.PrefetchScalarGridSpec(
        num_scalar_prefetch=1,
        grid=(n_tok // tm,),
        in_specs=[pl.BlockSpec((tm, dm), lambda i, pos: (i, 0)), pl.BlockSpec(memory_space=pl.ANY)],
        out_specs=pl.BlockSpec(memory_space=pl.ANY),
        scratch_shapes=[pltpu.SemaphoreType.DMA(())],
    )
    return pl.pallas_call(
        functools.partial(_dispatch_kernel, tok0),
        grid_spec=grid_spec,
        out_shape=jax.ShapeDtypeStruct(rows.shape, F32),
        input_output_aliases={2: 0},
        compiler_params=_params("arbitrary"),
        name="moe_dispatch",
    )(pos, x, rows)


def _ffn_kernel(be_ref, nv_ref, x_ref, wg_ref, wu_ref, wd_ref, o_ref):
    del be_ref

    @pl.when(pl.program_id(0) < nv_ref[0])
    def _():
        xb = x_ref[...].astype(BF16)
        g = _dot(xb, wg_ref[0, 0].astype(BF16))
        u = _dot(xb, wu_ref[0, 0].astype(BF16))
        hid = g * (1.0 / (1.0 + jnp.exp(-g))) * u
        o_ref[...] = _dot(hid.astype(BF16), wd_ref[0, 0].astype(BF16))

    @pl.when(pl.program_id(0) >= nv_ref[0])
    def _():
        o_ref[...] = jnp.zeros_like(o_ref)


def _expert_ffn(rows, blk_exp, n_valid, wg, wu, wd, layer):
    n_rows, dm = rows.shape
    hid = wg.shape[3]
    grid_spec = pltpu.PrefetchScalarGridSpec(
        num_scalar_prefetch=2,
        grid=(n_rows // MOE_ROWS,),
        in_specs=[pl.BlockSpec((MOE_ROWS, dm), lambda i, be, nv: (i, 0)),
                  pl.BlockSpec((1, 1, dm, hid), lambda i, be, nv: (layer, be[i], 0, 0)),
                  pl.BlockSpec((1, 1, dm, hid), lambda i, be, nv: (layer, be[i], 0, 0)),
                  pl.BlockSpec((1, 1, hid, dm), lambda i, be, nv: (layer, be[i], 0, 0))],
        out_specs=pl.BlockSpec((MOE_ROWS, dm), lambda i, be, nv: (i, 0)),
    )
    return pl.pallas_call(
        _ffn_kernel,
        grid_spec=grid_spec,
        out_shape=jax.ShapeDtypeStruct((n_rows, dm), F32),
        compiler_params=_params("arbitrary"),
        name="expert_ffn",
    )(blk_exp, n_valid, rows, wg, wu, wd)


def _combine_ln_kernel(tok0, pos_ref, x_ref, gate_ref, rows_ref, g_ref, b_ref, o_ref, buf, sem):
    tm = x_ref.shape[0]
    slot0 = (tok0 + pl.program_id(0) * tm) * MOE_TOPK
    _row_copies(pos_ref, slot0, tm, lambda r, k, src: pltpu.make_async_copy(
        rows_ref.at[pl.ds(src, 1), :], buf.at[k, pl.ds(r, 1), :], sem))
    gates = gate_ref[...]
    y = ALPHA * x_ref[...]
    for k in range(MOE_TOPK):
        y = y + gates[:, k:k + 1] * buf[k]
    o_ref[...] = _layer_norm(y, g_ref, b_ref)


def _combine_ln(pos, x, gates, rows, g, b, tok0):
    n_tok, dm = x.shape
    tm = 256
    row = pl.BlockSpec((tm, dm), lambda i, pos: (i, 0))
    vec = pl.BlockSpec((1, dm), lambda i, pos: (0, 0))
    grid_spec = pltpu.PrefetchScalarGridSpec(
        num_scalar_prefetch=1,
        grid=(n_tok // tm,),
        in_specs=[row, pl.BlockSpec((tm, HEAD_DIM), lambda i, pos: (i, 0)), pl.BlockSpec(memory_space=pl.ANY), vec, vec],
        out_specs=row,
        scratch_shapes=[pltpu.VMEM((MOE_TOPK, tm, dm), F32), pltpu.SemaphoreType.DMA(())],
    )
    return pl.pallas_call(
        functools.partial(_combine_ln_kernel, tok0),
        grid_spec=grid_spec,
        out_shape=jax.ShapeDtypeStruct((n_tok, dm), F32),
        compiler_params=_params("arbitrary"),
        name="moe_combine_ln",
    )(pos, x, gates, rows, g.reshape(1, dm), b.reshape(1, dm))


def _moe_ln(xp, xs, w_rg, b_rg, w_re, b_re, wg, wu, wd, layer, ln_g, ln_b):
    dm = xp.shape[1]
    n_grp, _, epg = w_re.shape
    n_exp = wg.shape[1]
    n_col = n_grp + n_grp * epg
    w = jnp.concatenate([w_rg, w_re.transpose(1, 0, 2).reshape(dm, n_grp * epg)], axis=1)
    w = jnp.pad(w, ((0, 0), (0, HEAD_DIM - n_col)))
    b = jnp.pad(jnp.concatenate([b_rg, b_re.reshape(-1)]), (0, HEAD_DIM - n_col)).reshape(1, HEAD_DIM)
    eid_p, gate_p = _router(xp, w, b, n_grp, epg)
    eid_s, gate_s = _router(xs, w, b, n_grp, epg)
    eid = jnp.concatenate([eid_p, eid_s], axis=0)
    rank, counts = _rank(eid)
    counts = counts[0, :n_exp].astype(jnp.int32)
    pcounts = (counts + MOE_ROWS - 1) // MOE_ROWS * MOE_ROWS
    pends = jnp.cumsum(pcounts)
    pstarts = pends - pcounts
    choice = eid[:, :MOE_TOPK]
    start_of = jnp.sum(jnp.where(choice[..., None] == jnp.arange(n_exp, dtype=jnp.int32), pstarts, 0), axis=-1)
    pos = (rank[:, :MOE_TOPK] + start_of).reshape(-1)
    n_slot = pos.shape[0]
    n_blk = -(-(n_slot + n_exp * (MOE_ROWS - 1)) // MOE_ROWS)
    blk_start = jnp.arange(n_blk, dtype=jnp.int32) * MOE_ROWS
    blk_exp = jnp.minimum(jnp.sum(blk_start[:, None] >= pends[None, :], axis=1), n_exp - 1).astype(jnp.int32)
    n_valid = (pends[-1] // MOE_ROWS).astype(jnp.int32).reshape(1)
    rows = jnp.zeros((n_blk * MOE_ROWS, dm), F32)
    rows = _dispatch(pos, xp, rows, 0)
    rows = _dispatch(pos, xs, rows, xp.shape[0])
    rows = _expert_ffn(rows, blk_exp, n_valid, wg, wu, wd, layer)
    return (_combine_ln(pos, xp, gate_p, rows, ln_g, ln_b, 0),
            _combine_ln(pos, xs, gate_s, rows, ln_g, ln_b, xp.shape[0]))


def kernel(x_prompt, x_sample, cache_kv_ab, state_kv_c_w128, state_kv_c_w512, state_kv_c_w2048, page_table, rel_bias,
           w_in_ab, w_out_ab, w_in_c, w_out_c, ln_mix_g, ln_mix_b, w_route_grp, b_route_grp, w_route_exp, b_route_exp,
           w_gate, w_up, w_down, ln_ffn_g, ln_ffn_b):
    bsz, seq, dm = x_prompt.shape
    nseq, t_len, _ = x_sample.shape
    n_pages = page_table.shape[1]
    past_len = n_pages * PAGE_SIZE
    c_states = (state_kv_c_w128, state_kv_c_w512, state_kv_c_w2048)
    xp = x_prompt.reshape(bsz * seq, dm)
    xs = x_sample.reshape(nseq * t_len, dm)
    d_ab = AB_HEADS * HEAD_DIM
    d_c = C_HEADS * HEAD_DIM
    d_cg = C_GROUP_HEADS * HEAD_DIM
    outs = {}

    for layer in range(DEPTH):
        li = layer // 2
        if layer % 2 == 0:
            w_in = w_in_ab[li].astype(BF16)
            w_out = w_out_ab[li].astype(BF16)
            qp = _matmul(xp, w_in, 0, 1, d_ab)
            kvp = _matmul(xp, w_in, d_ab // 512, 1, 2 * d_ab)
            qs = _matmul(xs, w_in, 0, 1, d_ab)
            kvs = _matmul(xs, w_in, d_ab // 512, 1, 2 * d_ab)
            att_p = _ab_prompt(qp, kvp, _moba_prompt_bias(rel_bias), bsz, seq)
            cache = cache_kv_ab[li].reshape(cache_kv_ab.shape[1], PAGE_SIZE, 4, SUBLANES, HEAD_DIM)
            tab, tabn = _moba_sample_tables(rel_bias, n_pages, t_len, past_len)
            att_s = _ab_sample(qs, kvs, cache, page_table, tab, tabn, t_len)
            mix_p = _matmul(att_p, w_out, 0, 1, dm)
            mix_s = _matmul(att_s, w_out, 0, 1, dm)
            outs["kv_ab_p"] = kvp.reshape(1, bsz, seq, 2, AB_HEADS, HEAD_DIM)
            outs["kv_ab_s"] = kvs.reshape(1, nseq, t_len, 2, AB_HEADS, HEAD_DIM)
        else:
            w_in = w_in_c[li].astype(BF16)
            w_out = w_out_c[li].astype(BF16)
            qp = _matmul(xp, w_in, 0, 1, d_c)
            qs = _matmul(xs, w_in, 0, 1, d_c)
            kvp, kvs = [], []
            for g in range(len(C_GROUPS)):
                kvp.append(_matmul(xp, w_in, d_c // 512 + g, d_c // 512, 2 * d_cg))
                kvs.append(_matmul(xs, w_in, d_c // 512 + g, d_c // 512, 2 * d_cg))
            att_p = _c_prompt(qp, kvp, _c_prompt_masks(rel_bias, seq), bsz, seq)
            q_pad = jnp.pad(qs.reshape(nseq * t_len, len(C_GROUPS), C_GROUP_HEADS, HEAD_DIM),
                            ((0, 0), (0, 0), (0, 8 - C_GROUP_HEADS), (0, 0)))
            kvn3 = [kv.reshape(nseq * t_len, 8, HEAD_DIM) for kv in kvs]
            st8 = [st[li][None].reshape(1, nseq, st.shape[2], 8, HEAD_DIM) for st in c_states]
            att_s, *next_states = _c_sample(q_pad, kvn3, st8, _c_sample_masks(rel_bias), t_len)
            att_s = att_s[:, :, :C_GROUP_HEADS].reshape(nseq * t_len, d_cg)
            mix_p = _matmul(att_p, w_out, 0, 1, dm)
            mix_s = _matmul(att_s, w_out, 0, 1, dm)
            for g, (win, _) in enumerate(C_GROUPS):
                keep = min(win, seq)
                kv5 = kvp[g].reshape(bsz, seq, 2, C_GROUP_HEADS, HEAD_DIM)
                outs[f"c_p{g}"] = kv5[:, seq - keep:][None]
                outs[f"c_s{g}"] = next_states[g].reshape((1, nseq, win, 2, C_GROUP_HEADS, HEAD_DIM))
        xp = _res_ln(xp, mix_p, ln_mix_g[layer], ln_mix_b[layer])
        xs = _res_ln(xs, mix_s, ln_mix_g[layer], ln_mix_b[layer])
        xp, xs = _moe_ln(xp, xs, w_route_grp[layer], b_route_grp[layer], w_route_exp[layer], b_route_exp[layer],
                         w_gate, w_up, w_down, layer, ln_ffn_g[layer], ln_ffn_b[layer])

    return (xp.reshape(bsz, seq, dm), xs.reshape(nseq, t_len, dm), outs["kv_ab_p"], outs["kv_ab_s"],
            outs["c_p0"], outs["c_s0"], outs["c_p1"], outs["c_s1"], outs["c_p2"], outs["c_s2"])
```
